```python
import jax, jax.numpy as jnp
from jax import lax
import numpy as np

D_MODEL = 2048
BATCH = 8
SEQ = 4096
DEPTH = 4

N_MIXERS = 2
N_SSD_LAYERS = (DEPTH + N_MIXERS - 1) // N_MIXERS
N_SC_LAYERS = DEPTH // N_MIXERS

SSM_EXPAND = 2
D_INNER = SSM_EXPAND * D_MODEL
SSM_HEAD_DIM = 64
SSM_HEADS = D_INNER // SSM_HEAD_DIM
SSM_GROUPS = 8
SSM_HEADS_PER_GROUP = SSM_HEADS // SSM_GROUPS
SSM_STATE = 128
SSM_CONV = 4
SSM_CHUNK = 128
SSM_BC_DIM = SSM_GROUPS * SSM_STATE
SSM_CONV_DIM = D_INNER + 2 * SSM_BC_DIM
SSM_IN_DIM = D_INNER + SSM_CONV_DIM + SSM_HEADS

SC_DIM = D_MODEL
SC_WIDTH = 3

D_FF = 5632
FFN_CONV = 3

EPS = 1e-5
DT_MIN = 1e-3
DT_MAX = 1e-1
A_MIN = 1.0
A_MAX = 16.0

kernel_name = "hybrid_ssd_shortconv_convffn_trunk"


def rms_norm(x, w):
    x32 = x.astype(jnp.float32)
    y = x32 * lax.rsqrt(jnp.mean(x32 * x32, axis=-1, keepdims=True) + EPS)
    return (y * w.astype(jnp.float32)).astype(x.dtype)


def causal_dwconv(x, w, b=None):
    width, ch = w.shape
    y = lax.conv_general_dilated(
        x, w[:, None, :].astype(x.dtype), window_strides=(1,),
        padding=[(width - 1, 0)], dimension_numbers=("NWC", "WIO", "NWC"),
        feature_group_count=ch)
    if b is not None:
        y = y + b.astype(x.dtype)
    return y


def segsum_exp(a):
    t = a.shape[-1]
    cs = jnp.cumsum(a, axis=-1)
    diff = cs[..., :, None] - cs[..., None, :]
    mask = jnp.tril(jnp.ones((t, t), dtype=bool))
    return jnp.exp(jnp.where(mask, diff, -jnp.inf))


def ssd_chunked(xh, dt, a, bm, cm):
    f32 = jnp.float32
    b_, s_, h_, p_ = xh.shape
    l_ = SSM_CHUNK
    c_ = s_ // l_
    g_, k_, n_ = SSM_GROUPS, SSM_HEADS_PER_GROUP, SSM_STATE
    xdt = (xh.astype(f32) * dt[..., None]).reshape(b_, c_, l_, g_, k_, p_)
    da = jnp.moveaxis((dt * a).reshape(b_, c_, l_, g_, k_), 2, -1)
    a_cs = jnp.cumsum(da, axis=-1)
    bc = bm.astype(f32).reshape(b_, c_, l_, g_, n_)
    cc = cm.astype(f32).reshape(b_, c_, l_, g_, n_)
    cb = jnp.einsum("bclgn,bcsgn->bcgls", cc, bc)
    decay = segsum_exp(da)
    y_diag = jnp.einsum("bcgls,bcgkls,bcsgkp->bclgkp", cb, decay, xdt)
    decay_to_end = jnp.exp(a_cs[..., -1:] - a_cs)
    states = jnp.einsum("bclgn,bcgkl,bclgkp->bcgkpn", bc, decay_to_end, xdt)
    chunk_decay = jnp.exp(a_cs[..., -1])

    def step(hstate, inp):
        st, dec = inp
        return hstate * dec[..., None, None] + st, hstate

    h0 = jnp.zeros((b_, g_, k_, p_, n_), f32)
    _, prev = lax.scan(step, h0, (jnp.moveaxis(states, 1, 0), jnp.moveaxis(chunk_decay, 1, 0)))
    prev = jnp.moveaxis(prev, 0, 1)
    y_off = jnp.einsum("bclgn,bcgkpn,bcgkl->bclgkp", cc, prev, jnp.exp(a_cs))
    return (y_diag + y_off).reshape(b_, s_, h_, p_)


def ssd_mixer(u, w_in, conv_w, conv_b, dt_bias, a_log, d_skip, norm_w, w_out):
    f32 = jnp.float32
    b_, s_, _ = u.shape
    zxbcdt = u @ w_in
    z, xbc, dt_raw = jnp.split(zxbcdt, [D_INNER, D_INNER + SSM_CONV_DIM], axis=-1)
    xbc = jax.nn.silu(causal_dwconv(xbc, conv_w, conv_b))
    xs, bm, cm = jnp.split(xbc, [D_INNER, D_INNER + SSM_BC_DIM], axis=-1)
    dt = jax.nn.softplus(dt_raw.astype(f32) + dt_bias.astype(f32))
    a = -jnp.exp(a_log.astype(f32))
    xh = xs.reshape(b_, s_, SSM_HEADS, SSM_HEAD_DIM)
    y = ssd_chunked(xh, dt, a,
                    bm.reshape(b_, s_, SSM_GROUPS, SSM_STATE),
                    cm.reshape(b_, s_, SSM_GROUPS, SSM_STATE))
    y = y + xh.astype(f32) * d_skip.astype(f32)[:, None]
    yg = (y.reshape(b_, s_, D_INNER) * jax.nn.silu(z.astype(f32)))
    yg = yg.reshape(b_, s_, SSM_GROUPS, D_INNER // SSM_GROUPS)
    yg = yg * lax.rsqrt(jnp.mean(yg * yg, axis=-1, keepdims=True) + EPS)
    y = (yg.reshape(b_, s_, D_INNER) * norm_w.astype(f32)).astype(u.dtype)
    return y @ w_out


def short_conv_mixer(u, w_in, conv_w, w_out):
    bg, cg, h = jnp.split(u @ w_in, 3, axis=-1)
    return (bg * causal_dwconv(cg * h, conv_w)) @ w_out


def conv_ffn(u, w_up, conv_w, conv_b, w_down):
    hu = causal_dwconv(u @ w_up, conv_w, conv_b)
    g, v = jnp.split(hu, 2, axis=-1)
    return (jax.nn.silu(g) * v) @ w_down


def setup_inputs(seed: int = 0) -> dict:
    key = jax.random.key(seed)
    ks = jax.random.split(key, 24)
    f32 = jnp.float32
    nrm = lambda k, shape, scale: jax.random.normal(k, shape, f32) * scale
    La, Lb = N_SSD_LAYERS, N_SC_LAYERS
    dt0 = jnp.exp(jax.random.uniform(ks[5], (La, SSM_HEADS), f32,
                                     float(np.log(DT_MIN)), float(np.log(DT_MAX))))
    return {
        "x": jax.random.normal(ks[0], (BATCH, SEQ, D_MODEL), f32),
        "mix_norm_w": 1.0 + nrm(ks[1], (DEPTH, D_MODEL), 0.02),
        "ffn_norm_w": 1.0 + nrm(ks[2], (DEPTH, D_MODEL), 0.02),
        "final_norm_w": 1.0 + nrm(ks[3], (D_MODEL,), 0.02),
        "ssd_w_in": nrm(ks[4], (La, D_MODEL, SSM_IN_DIM), D_MODEL ** -0.5),
        "ssd_conv_w": nrm(ks[6], (La, SSM_CONV, SSM_CONV_DIM), SSM_CONV ** -0.5),
        "ssd_conv_b": nrm(ks[7], (La, SSM_CONV_DIM), 0.02),
        "ssd_dt_bias": dt0 + jnp.log(-jnp.expm1(-dt0)),
        "ssd_a_log": jnp.log(jax.random.uniform(ks[8], (La, SSM_HEADS), f32, A_MIN, A_MAX)),
        "ssd_d": 1.0 + nrm(ks[9], (La, SSM_HEADS), 0.02),
        "ssd_norm_w": 1.0 + nrm(ks[10], (La, D_INNER), 0.02),
        "ssd_w_out": nrm(ks[11], (La, D_INNER, D_MODEL), D_INNER ** -0.5),
        "sc_w_in": nrm(ks[12], (Lb, D_MODEL, 3 * SC_DIM), D_MODEL ** -0.5),
        "sc_conv_w": nrm(ks[13], (Lb, SC_WIDTH, SC_DIM), SC_WIDTH ** -0.5),
        "sc_w_out": nrm(ks[14], (Lb, SC_DIM, D_MODEL), SC_DIM ** -0.5),
        "ffn_w_up": nrm(ks[15], (DEPTH, D_MODEL, 2 * D_FF), D_MODEL ** -0.5),
        "ffn_conv_w": nrm(ks[16], (DEPTH, FFN_CONV, 2 * D_FF), FFN_CONV ** -0.5),
        "ffn_conv_b": nrm(ks[17], (DEPTH, 2 * D_FF), 0.02),
        "ffn_w_down": nrm(ks[18], (DEPTH, D_FF, D_MODEL), D_FF ** -0.5),
    }


def reference(x, mix_norm_w, ffn_norm_w, final_norm_w,
              ssd_w_in, ssd_conv_w, ssd_conv_b, ssd_dt_bias, ssd_a_log, ssd_d,
              ssd_norm_w, ssd_w_out,
              sc_w_in, sc_conv_w, sc_w_out,
              ffn_w_up, ffn_conv_w, ffn_conv_b, ffn_w_down):
    for i in range(DEPTH):
        j = i // N_MIXERS
        h = rms_norm(x, mix_norm_w[i])
        if i % N_MIXERS == 0:
            x = x + ssd_mixer(h, ssd_w_in[j], ssd_conv_w[j], ssd_conv_b[j], ssd_dt_bias[j],
                              ssd_a_log[j], ssd_d[j], ssd_norm_w[j], ssd_w_out[j])
        else:
            x = x + short_conv_mixer(h, sc_w_in[j], sc_conv_w[j], sc_w_out[j])
        h = rms_norm(x, ffn_norm_w[i])
        x = x + conv_ffn(h, ffn_w_up[i], ffn_conv_w[i], ffn_conv_b[i], ffn_w_down[i])
    return rms_norm(x, final_norm_w)
```

```python
import functools

import jax
import jax.numpy as jnp
from jax import lax
from jax.experimental import pallas as pl
from jax.experimental.pallas import tpu as pltpu

F32 = jnp.float32
BF16 = jnp.bfloat16

EPS = 1e-5
SSM_HEAD_DIM = 64
SSM_GROUPS = 8
SSM_STATE = 128
SSM_CHUNK = 128
SSM_CONV = 4
SC_WIDTH = 3
FFN_CONV = 3

LANES = 128
SUBLANES = 8
VMEM_LIMIT_BYTES = 56 * 1024 * 1024

HALO = SUBLANES


def _sigmoid(v):
    return 1.0 / (1.0 + jnp.exp(-v))


def _rmsnorm_rows(x_ref, w_ref, out_ref, rows_per_step=64):
    n_rows = x_ref.shape[0]
    w = w_ref[...]

    def body(r, carry):
        sl = pl.ds(pl.multiple_of(r * rows_per_step, rows_per_step), rows_per_step)
        x = x_ref[sl, :]
        ms = jnp.mean(x * x, axis=-1, keepdims=True)
        out_ref[sl, :] = (x * lax.rsqrt(ms + EPS) * w).astype(out_ref.dtype)
        return carry

    lax.fori_loop(0, n_rows // rows_per_step, body, 0)


def _causal_conv(buf_ref, w, n_rows, width):
    acc = None
    for k in range(width):
        start = HALO - (width - 1) + k
        term = w[k:k + 1, :] * buf_ref[start:start + n_rows, :]
        acc = term if acc is None else acc + term
    return acc


def _norm_inproj_kernel(x_ref, nw_ref, w_ref, wdt_ref, o_ref, dt_ref, h_ref):
    @pl.when(pl.program_id(1) == 0)
    def _():
        _rmsnorm_rows(x_ref, nw_ref, h_ref)
        dt_ref[...] = jnp.dot(h_ref[...], wdt_ref[...], preferred_element_type=F32)

    o_ref[...] = jnp.dot(h_ref[...], w_ref[...], preferred_element_type=F32).astype(o_ref.dtype)


def _norm_inproj(x, nw, w, wdt, *, tm, tn):
    t, d = x.shape
    n = w.shape[1]
    return pl.pallas_call(
        _norm_inproj_kernel,
        grid=(t // tm, n // tn),
        in_specs=[
            pl.BlockSpec((tm, d), lambda i, j: (i, 0)),
            pl.BlockSpec((1, d), lambda i, j: (0, 0)),
            pl.BlockSpec((d, tn), lambda i, j: (0, j)),
            pl.BlockSpec((d, LANES), lambda i, j: (0, 0)),
        ],
        out_specs=[
            pl.BlockSpec((tm, tn), lambda i, j: (i, j)),
            pl.BlockSpec((tm, LANES), lambda i, j: (i, 0)),
        ],
        out_shape=[
            jax.ShapeDtypeStruct((t, n), BF16),
            jax.ShapeDtypeStruct((t, LANES), F32),
        ],
        scratch_shapes=[pltpu.VMEM((tm, d), BF16)],
        compiler_params=pltpu.CompilerParams(
            dimension_semantics=("arbitrary", "arbitrary"),
            vmem_limit_bytes=VMEM_LIMIT_BYTES),
        name="ssd_norm_inproj",
    )(x, nw, w, wdt)


def _split_bf16(v, parts):
    out = []
    rem = v
    for _ in range(parts):
        p = rem.astype(BF16)
        out.append(p)
        rem = rem - p.astype(F32)
    return out


def _ssd_core_kernel(z_ref, x_ref, bc_ref, dt_ref, cwx_ref, cbx_ref, cwbc_ref, cbbc_ref,
                     dtb_ref, alog_ref, dexp_ref, nw_ref, expand_ref, o_ref,
                     state_ref, xbuf_ref, bcbuf_ref, xc_ref, bcc_ref, xdt_ref, xw_ref, ex_ref,
                     *, n_groups, heads_per_group, head_dim, n_state):
    L = SSM_CHUNK
    gw = heads_per_group * head_dim
    bdim = n_groups * n_state

    @pl.when(pl.program_id(1) == 0)
    def _():
        state_ref[...] = jnp.zeros_like(state_ref)
        xbuf_ref[0:HALO, :] = jnp.zeros((HALO, xbuf_ref.shape[1]), F32)
        bcbuf_ref[0:HALO, :] = jnp.zeros((HALO, bcbuf_ref.shape[1]), F32)

    xbuf_ref[HALO:HALO + L, :] = x_ref[...].astype(F32)
    bcbuf_ref[HALO:HALO + L, :] = bc_ref[...].astype(F32)
    xa = _causal_conv(xbuf_ref, cwx_ref[...], L, SSM_CONV) + cbx_ref[...]
    xc_ref[...] = xa * _sigmoid(xa)
    ba = _causal_conv(bcbuf_ref, cwbc_ref[...], L, SSM_CONV) + cbbc_ref[...]
    bcc_ref[...] = (ba * _sigmoid(ba)).astype(BF16)
    xbuf_ref[0:HALO, :] = xbuf_ref[L:L + HALO, :]
    bcbuf_ref[0:HALO, :] = bcbuf_ref[L:L + HALO, :]

    dtr = dt_ref[...] + dtb_ref[...]
    dtv = jnp.maximum(dtr, 0.0) + jnp.log(1.0 + jnp.exp(-jnp.abs(dtr)))
    a = -jnp.exp(alog_ref[...])
    da = dtv * a

    row = lax.broadcasted_iota(jnp.int32, (L, L), 0)
    col = lax.broadcasted_iota(jnp.int32, (L, L), 1)
    causal = row >= col
    tril = jnp.where(causal, 1.0, 0.0).astype(BF16)
    acs = jnp.dot(jnp.concatenate([tril, tril, tril], axis=1),
                  jnp.concatenate(_split_bf16(da, 3), axis=0),
                  preferred_element_type=F32)
    acs_t = acs.T
    last = acs[L - 1:L, :]
    dte = jnp.exp(last - acs)
    cdec = jnp.broadcast_to(jnp.exp(last), (2 * SUBLANES, LANES))

    q = jnp.concatenate([dtv, dtv * dte, cdec], axis=0)
    ex_ref[...] = jnp.dot(jnp.concatenate(_split_bf16(q, 2), axis=1), expand_ref[...],
                          preferred_element_type=F32)
    xc = xc_ref[...]
    xdt_ref[...] = (xc * ex_ref[0:L, :]).astype(BF16)
    xw_ref[...] = (xc * ex_ref[L:2 * L, :]).astype(BF16)

    lane = lax.broadcasted_iota(jnp.int32, (L, LANES), 1)
    lo_half = lane < head_dim
    heads_per_tile = LANES // head_dim
    tiles_per_group = gw // LANES
    neg_inf = jnp.float32(-jnp.inf)

    for g in range(n_groups):
        gs = slice(g * gw, (g + 1) * gw)
        bg = bcc_ref[:, g * n_state:(g + 1) * n_state]
        cg = bcc_ref[:, bdim + g * n_state:bdim + (g + 1) * n_state]
        cb = lax.dot_general(cg, bg, (((1,), (1,)), ((), ())), preferred_element_type=F32)
        cg32 = cg.astype(F32)
        prev = state_ref[:, gs]
        prev_bf = prev.astype(BF16)
        y_tiles = []
        for tl in range(tiles_per_group):
            lhs_parts = []
            rhs_parts = []
            xp = xdt_ref[:, g * gw + tl * LANES:g * gw + (tl + 1) * LANES]
            pp = prev_bf[:, tl * LANES:(tl + 1) * LANES]
            for hh in range(heads_per_tile):
                h = g * heads_per_group + tl * heads_per_tile + hh
                acol = acs[:, h:h + 1]
                arow = acs_t[h:h + 1, :]
                decay = jnp.exp(jnp.where(causal, acol - arow, neg_inf))
                lhs_parts.append((cb * decay).astype(BF16))
                lhs_parts.append((cg32 * jnp.exp(acol)).astype(BF16))
                keep = lo_half if hh == 0 else jnp.logical_not(lo_half)
                rhs_parts.append(jnp.where(keep, xp, jnp.zeros_like(xp)))
                rhs_parts.append(jnp.where(keep, pp, jnp.zeros_like(pp)))
            y_tiles.append(jnp.dot(jnp.concatenate(lhs_parts, axis=1),
                                   jnp.concatenate(rhs_parts, axis=0),
                                   preferred_element_type=F32))
        y = jnp.concatenate(y_tiles, axis=1)

        s_new = lax.dot_general(bg, xw_ref[:, gs], (((0,), (0,)), ((), ())),
                                preferred_element_type=F32)
        state_ref[:, gs] = prev * ex_ref[2 * L:2 * L + 1, gs] + s_new

        yv = y + xc_ref[:, gs] * dexp_ref[:, gs]
        zg = z_ref[:, gs].astype(F32)
        yv = yv * (zg * _sigmoid(zg))
        ms = jnp.mean(yv * yv, axis=-1, keepdims=True)
        o_ref[:, gs] = (yv * lax.rsqrt(ms + EPS) * nw_ref[:, gs]).astype(o_ref.dtype)


def _ssd_core(zxbc, dt_raw, conv_w, conv_b, dt_bias, a_log, d_skip, norm_w, *,
              batch, seq, d_inner, n_groups, n_state, head_dim):
    t = zxbc.shape[0]
    n_heads = d_inner // head_dim
    heads_per_group = n_heads // n_groups
    bdim = n_groups * n_state
    L = SSM_CHUNK
    chunks = seq // L
    assert d_inner % (2 * bdim) == 0 and LANES % head_dim == 0 and n_heads <= LANES

    cwx, cwbc = conv_w[:, :d_inner], conv_w[:, d_inner:]
    cbx, cbbc = conv_b[None, :d_inner], conv_b[None, d_inner:]
    pad = LANES - n_heads
    dtb = jnp.pad(dt_bias, (0, pad))[None, :]
    alog = jnp.pad(a_log, (0, pad))[None, :]
    dexp = jnp.repeat(d_skip, head_dim)[None, :]
    nw = norm_w[None, :]
    e = (jnp.arange(LANES)[:, None] == (jnp.arange(d_inner) // head_dim)[None, :]).astype(BF16)
    expand = jnp.concatenate([e, e], axis=0)

    row_map = lambda b, c: (b * chunks + c, 0)
    const = lambda b, c: (0, 0)
    kern = functools.partial(_ssd_core_kernel, n_groups=n_groups, heads_per_group=heads_per_group,
                             head_dim=head_dim, n_state=n_state)
    return pl.pallas_call(
        kern,
        grid=(batch, chunks),
        in_specs=[
            pl.BlockSpec((L, d_inner), row_map),
            pl.BlockSpec((L, d_inner), lambda b, c: (b * chunks + c, 1)),
            pl.BlockSpec((L, 2 * bdim), lambda b, c: (b * chunks + c, d_inner // bdim)),
            pl.BlockSpec((L, LANES), row_map),
            pl.BlockSpec((SSM_CONV, d_inner), const),
            pl.BlockSpec((1, d_inner), const),
            pl.BlockSpec((SSM_CONV, 2 * bdim), const),
            pl.BlockSpec((1, 2 * bdim), const),
            pl.BlockSpec((1, LANES), const),
            pl.BlockSpec((1, LANES), const),
            pl.BlockSpec((1, d_inner), const),
            pl.BlockSpec((1, d_inner), const),
            pl.BlockSpec((2 * LANES, d_inner), const),
        ],
        out_specs=pl.BlockSpec((L, d_inner), row_map),
        out_shape=jax.ShapeDtypeStruct((t, d_inner), BF16),
        scratch_shapes=[
            pltpu.VMEM((n_state, d_inner), F32),
            pltpu.VMEM((HALO + L, d_inner), F32),
            pltpu.VMEM((HALO + L, 2 * bdim), F32),
            pltpu.VMEM((L, d_inner), F32),
            pltpu.VMEM((L, 2 * bdim), BF16),
            pltpu.VMEM((L, d_inner), BF16),
            pltpu.VMEM((L, d_inner), BF16),
            pltpu.VMEM((2 * L + 2 * SUBLANES, d_inner), F32),
        ],
        compiler_params=pltpu.CompilerParams(
            dimension_semantics=("arbitrary", "arbitrary"),
            vmem_limit_bytes=VMEM_LIMIT_BYTES),
        name="ssd_core",
    )(zxbc, zxbc, zxbc, dt_raw, cwx, cbx, cwbc, cbbc, dtb, alog, dexp, nw, expand)


def _proj_residual_kernel(x_ref, y_ref, w_ref, o_ref):
    o_ref[...] = x_ref[...] + jnp.dot(y_ref[...], w_ref[...], preferred_element_type=F32)


def _proj_residual(x, y, w, *, tm, tn):
    t, d = x.shape
    k = y.shape[1]
    return pl.pallas_call(
        _proj_residual_kernel,
        grid=(t // tm, d // tn),
        in_specs=[
            pl.BlockSpec((tm, tn), lambda i, j: (i, j)),
            pl.BlockSpec((tm, k), lambda i, j: (i, 0)),
            pl.BlockSpec((k, tn), lambda i, j: (0, j)),
        ],
        out_specs=pl.BlockSpec((tm, tn), lambda i, j: (i, j)),
        out_shape=jax.ShapeDtypeStruct((t, d), F32),
        compiler_params=pltpu.CompilerParams(
            dimension_semantics=("arbitrary", "arbitrary"),
            vmem_limit_bytes=VMEM_LIMIT_BYTES),
        name="ssd_outproj_residual",
    )(x, y, w)


def _load_halo(buf_ref, halo_ref, tiles_per_seq):
    i = pl.program_id(0)
    j = pl.program_id(1)

    @pl.when(i % tiles_per_seq == 0)
    def _():
        buf_ref[0:HALO, :] = jnp.zeros((HALO, buf_ref.shape[1]), F32)

    @pl.when(i % tiles_per_seq != 0)
    def _():
        buf_ref[0:HALO, :] = halo_ref[j]


def _finish_tile(o_ref, fnw_ref, final_norm):
    if final_norm:
        @pl.when(pl.program_id(1) == pl.num_programs(1) - 1)
        def _():
            _rmsnorm_rows(o_ref, fnw_ref, o_ref)


def _ffn_kernel(x_ref, nw_ref, wg_ref, wv_ref, cwg_ref, cwv_ref, cbg_ref, cbv_ref, w2_ref, fnw_ref,
                o_ref, h_ref, gbuf_ref, vbuf_ref, ghalo_ref, vhalo_ref, *, tiles_per_seq, final_norm):
    j = pl.program_id(1)
    tm = x_ref.shape[0]

    @pl.when(j == 0)
    def _():
        _rmsnorm_rows(x_ref, nw_ref, h_ref)
        o_ref[...] = x_ref[...]

    h = h_ref[...]
    gbuf_ref[HALO:HALO + tm, :] = jnp.dot(h, wg_ref[...], preferred_element_type=F32)
    vbuf_ref[HALO:HALO + tm, :] = jnp.dot(h, wv_ref[...], preferred_element_type=F32)
    _load_halo(gbuf_ref, ghalo_ref, tiles_per_seq)
    _load_halo(vbuf_ref, vhalo_ref, tiles_per_seq)
    g = _causal_conv(gbuf_ref, cwg_ref[...], tm, FFN_CONV) + cbg_ref[...]
    v = _causal_conv(vbuf_ref, cwv_ref[...], tm, FFN_CONV) + cbv_ref[...]
    ghalo_ref[j] = gbuf_ref[tm:tm + HALO, :]
    vhalo_ref[j] = vbuf_ref[tm:tm + HALO, :]
    act = (g * _sigmoid(g) * v).astype(BF16)
    o_ref[...] += jnp.dot(act, w2_ref[...], preferred_element_type=F32)
    _finish_tile(o_ref, fnw_ref, final_norm)


def _conv_ffn(x, nw, w_up, conv_w, conv_b, w_down, fnw, *, seq, tm, tc, final_norm):
    t, d = x.shape
    d_ff = w_down.shape[0]
    nchunks = d_ff // tc
    kern = functools.partial(_ffn_kernel, tiles_per_seq=seq // tm, final_norm=final_norm)
    return pl.pallas_call(
        kern,
        grid=(t // tm, nchunks),
        in_specs=[
            pl.BlockSpec((tm, d), lambda i, j: (i, 0)),
            pl.BlockSpec((1, d), lambda i, j: (0, 0)),
            pl.BlockSpec((d, tc), lambda i, j: (0, j)),
            pl.BlockSpec((d, tc), lambda i, j: (0, nchunks + j)),
            pl.BlockSpec((FFN_CONV, tc), lambda i, j: (0, j)),
            pl.BlockSpec((FFN_CONV, tc), lambda i, j: (0, nchunks + j)),
            pl.BlockSpec((1, tc), lambda i, j: (0, j)),
            pl.BlockSpec((1, tc), lambda i, j: (0, nchunks + j)),
            pl.BlockSpec((tc, d), lambda i, j: (j, 0)),
            pl.BlockSpec((1, d), lambda i, j: (0, 0)),
        ],
        out_specs=pl.BlockSpec((tm, d), lambda i, j: (i, 0)),
        out_shape=jax.ShapeDtypeStruct((t, d), F32),
        scratch_shapes=[
            pltpu.VMEM((tm, d), BF16),
            pltpu.VMEM((HALO + tm, tc), F32),
            pltpu.VMEM((HALO + tm, tc), F32),
            pltpu.VMEM((nchunks, HALO, tc), F32),
            pltpu.VMEM((nchunks, HALO, tc), F32),
        ],
        compiler_params=pltpu.CompilerParams(
            dimension_semantics=("arbitrary", "arbitrary"),
            vmem_limit_bytes=VMEM_LIMIT_BYTES),
        name="conv_ffn",
    )(x, nw, w_up, w_up, conv_w, conv_w, conv_b, conv_b, w_down, fnw)


def _sc_kernel(x_ref, nw_ref, wb_ref, wc_ref, wh_ref, cw_ref, w2_ref,
               o_ref, h_ref, ubuf_ref, uhalo_ref, *, tiles_per_seq):
    j = pl.program_id(1)
    tm = x_ref.shape[0]

    @pl.when(j == 0)
    def _():
        _rmsnorm_rows(x_ref, nw_ref, h_ref)
        o_ref[...] = x_ref[...]

    h = h_ref[...]
    cgate = jnp.dot(h, wc_ref[...], preferred_element_type=F32)
    hid = jnp.dot(h, wh_ref[...], preferred_element_type=F32)
    ubuf_ref[HALO:HALO + tm, :] = cgate * hid
    _load_halo(ubuf_ref, uhalo_ref, tiles_per_seq)
    cv = _causal_conv(ubuf_ref, cw_ref[...], tm, SC_WIDTH)
    uhalo_ref[j] = ubuf_ref[tm:tm + HALO, :]
    bgate = jnp.dot(h, wb_ref[...], preferred_element_type=F32)
    act = (bgate * cv).astype(BF16)
    o_ref[...] += jnp.dot(act, w2_ref[...], preferred_element_type=F32)


def _short_conv(x, nw, w_in, conv_w, w_out, *, seq, tm, tc):
    t, d = x.shape
    sc_dim = w_out.shape[0]
    nchunks = sc_dim // tc
    kern = functools.partial(_sc_kernel, tiles_per_seq=seq // tm)
    return pl.pallas_call(
        kern,
        grid=(t // tm, nchunks),
        in_specs=[
            pl.BlockSpec((tm, d), lambda i, j: (i, 0)),
            pl.BlockSpec((1, d), lambda i, j: (0, 0)),
            pl.BlockSpec((d, tc), lambda i, j: (0, j)),
            pl.BlockSpec((d, tc), lambda i, j: (0, nchunks + j)),
            pl.BlockSpec((d, tc), lambda i, j: (0, 2 * nchunks + j)),
            pl.BlockSpec((SC_WIDTH, tc), lambda i, j: (0, j)),
            pl.BlockSpec((tc, d), lambda i, j: (j, 0)),
        ],
        out_specs=pl.BlockSpec((tm, d), lambda i, j: (i, 0)),
        out_shape=jax.ShapeDtypeStruct((t, d), F32),
        scratch_shapes=[
            pltpu.VMEM((tm, d), BF16),
            pltpu.VMEM((HALO + tm, tc), F32),
            pltpu.VMEM((nchunks, HALO, tc), F32),
        ],
        compiler_params=pltpu.CompilerParams(
            dimension_semantics=("arbitrary", "arbitrary"),
            vmem_limit_bytes=VMEM_LIMIT_BYTES),
        name="short_conv_mixer",
    )(x, nw, w_in, w_in, w_in, conv_w, w_out)


def _tile(n, pref):
    c = min(pref, n)
    while n % c:
        c -= LANES
    return c


def _trunk(x, mix_norm_w, ffn_norm_w, final_norm_w,
           ssd_w_in, ssd_conv_w, ssd_conv_b, ssd_dt_bias, ssd_a_log, ssd_d, ssd_norm_w, ssd_w_out,
           sc_w_in, sc_conv_w, sc_w_out,
           ffn_w_up, ffn_conv_w, ffn_conv_b, ffn_w_down,
           *, head_dim=SSM_HEAD_DIM, n_groups=SSM_GROUPS, n_state=SSM_STATE,
           tm_proj=1024, tn_proj=1024, tm_mlp=512, tc_mlp=512):
    batch, seq, d_model = x.shape
    depth = mix_norm_w.shape[0]
    t = batch * seq
    d_inner = ssd_w_out.shape[1]
    n_heads = ssd_dt_bias.shape[1]
    main_cols = ssd_w_in.shape[2] - n_heads
    assert d_inner == n_heads * head_dim

    tm_proj = _tile(seq, tm_proj)
    tm_mlp = _tile(seq, tm_mlp)
    xs = x.reshape(t, d_model)
    for i in range(depth):
        j = i // 2
        nw = mix_norm_w[i][None, :]
        if i % 2 == 0:
            w_main = ssd_w_in[j, :, :main_cols].astype(BF16)
            w_dt = jnp.pad(ssd_w_in[j, :, main_cols:], ((0, 0), (0, LANES - n_heads))).astype(BF16)
            zxbc, dt_raw = _norm_inproj(xs, nw, w_main, w_dt, tm=tm_proj, tn=_tile(main_cols, tn_proj))
            y = _ssd_core(zxbc, dt_raw, ssd_conv_w[j], ssd_conv_b[j], ssd_dt_bias[j], ssd_a_log[j],
                          ssd_d[j], ssd_norm_w[j], batch=batch, seq=seq, d_inner=d_inner,
                          n_groups=n_groups, n_state=n_state, head_dim=head_dim)
            xs = _proj_residual(xs, y, ssd_w_out[j].astype(BF16), tm=tm_mlp, tn=_tile(d_model, tn_proj))
        else:
            xs = _short_conv(xs, nw, sc_w_in[j].astype(BF16), sc_conv_w[j], sc_w_out[j].astype(BF16),
                             seq=seq, tm=tm_mlp, tc=_tile(sc_w_out.shape[1], tc_mlp))
        xs = _conv_ffn(xs, ffn_norm_w[i][None, :], ffn_w_up[i].astype(BF16), ffn_conv_w[i],
                       ffn_conv_b[i][None, :], ffn_w_down[i].astype(BF16), final_norm_w[None, :],
                       seq=seq, tm=tm_mlp, tc=_tile(ffn_w_down.shape[1], tc_mlp),
                       final_norm=(i == depth - 1))
    return xs.reshape(batch, seq, d_model)


def kernel(x, mix_norm_w, ffn_norm_w, final_norm_w, ssd_w_in, ssd_conv_w, ssd_conv_b, ssd_dt_bias,
           ssd_a_log, ssd_d, ssd_norm_w, ssd_w_out, sc_w_in, sc_conv_w, sc_w_out,
           ffn_w_up, ffn_conv_w, ffn_conv_b, ffn_w_down):
    return _trunk(x, mix_norm_w, ffn_norm_w, final_norm_w,
                  ssd_w_in, ssd_conv_w, ssd_conv_b, ssd_dt_bias, ssd_a_log, ssd_d, ssd_norm_w, ssd_w_out,
                  sc_w_in, sc_conv_w, sc_w_out,
                  ffn_w_up, ffn_conv_w, ffn_conv_b, ffn_w_down)
```

```python
import functools

import jax
import jax.numpy as jnp
from jax import lax
from jax.experimental import pallas as pl
from jax.experimental.pallas import tpu as pltpu

F32 = jnp.float32
BF16 = jnp.bfloat16

EPS = 1e-5
SSM_HEAD_DIM = 64
SSM_GROUPS = 8
SSM_STATE = 128
SSM_CHUNK = 128
SSM_CONV = 4
SC_WIDTH = 3
FFN_CONV = 3

LANES = 128
SUBLANES = 8
VMEM_LIMIT_BYTES = 56 * 1024 * 1024

HALO = SUBLANES


def _sigmoid(v):
    return 1.0 / (1.0 + jnp.exp(-v))


def _rmsnorm_rows(x_ref, w_ref, out_ref, rows_per_step=64):
    n_rows = x_ref.shape[0]
    w = w_ref[...]

    def body(r, carry):
        sl = pl.ds(pl.multiple_of(r * rows_per_step, rows_per_step), rows_per_step)
        x = x_ref[sl, :]
        ms = jnp.mean(x * x, axis=-1, keepdims=True)
        out_ref[sl, :] = (x * lax.rsqrt(ms + EPS) * w).astype(out_ref.dtype)
        return carry

    lax.fori_loop(0, n_rows // rows_per_step, body, 0)


def _causal_conv(buf_ref, w, n_rows, width):
    acc = None
    for k in range(width):
        start = HALO - (width - 1) + k
        term = w[k:k + 1, :] * buf_ref[start:start + n_rows, :]
        acc = term if acc is None else acc + term
    return acc


def _norm_inproj_kernel(x_ref, nw_ref, w_ref, wdt_ref, o_ref, dt_ref, h_ref):
    @pl.when(pl.program_id(1) == 0)
    def _():
        _rmsnorm_rows(x_ref, nw_ref, h_ref)
        dt_ref[...] = jnp.dot(h_ref[...], wdt_ref[...], preferred_element_type=F32)

    o_ref[...] = jnp.dot(h_ref[...], w_ref[...], preferred_element_type=F32).astype(o_ref.dtype)


def _norm_inproj(x, nw, w, wdt, *, tm, tn):
    t, d = x.shape
    n = w.shape[1]
    return pl.pallas_call(
        _norm_inproj_kernel,
        grid=(t // tm, n // tn),
        in_specs=[
            pl.BlockSpec((tm, d), lambda i, j: (i, 0)),
            pl.BlockSpec((1, d), lambda i, j: (0, 0)),
            pl.BlockSpec((d, tn), lambda i, j: (0, j)),
            pl.BlockSpec((d, LANES), lambda i, j: (0, 0)),
        ],
        out_specs=[
            pl.BlockSpec((tm, tn), lambda i, j: (i, j)),
            pl.BlockSpec((tm, LANES), lambda i, j: (i, 0)),
        ],
        out_shape=[
            jax.ShapeDtypeStruct((t, n), BF16),
            jax.ShapeDtypeStruct((t, LANES), F32),
        ],
        scratch_shapes=[pltpu.VMEM((tm, d), BF16)],
        compiler_params=pltpu.CompilerParams(
            dimension_semantics=("arbitrary", "arbitrary"),
            vmem_limit_bytes=VMEM_LIMIT_BYTES),
        name="ssd_norm_inproj",
    )(x, nw, w, wdt)


def _split_bf16(v, parts):
    out = []
    rem = v
    for _ in range(parts):
        p = rem.astype(BF16)
        out.append(p)
        rem = rem - p.astype(F32)
    return out


def _ssd_core_kernel(z_ref, x_ref, bc_ref, dt_ref, cwx_ref, cbx_ref, cwbc_ref, cbbc_ref,
                     dtb_ref, alog_ref, dexp_ref, nw_ref, expand_ref, o_ref,
                     state_ref, xbuf_ref, bcbuf_ref, xc_ref, bcc_ref, xdt_ref, xw_ref, ex_ref,
                     *, n_groups, heads_per_group, head_dim, n_state):
    L = SSM_CHUNK
    gw = heads_per_group * head_dim
    bdim = n_groups * n_state

    @pl.when(pl.program_id(1) == 0)
    def _():
        state_ref[...] = jnp.zeros_like(state_ref)
        xbuf_ref[0:HALO, :] = jnp.zeros((HALO, xbuf_ref.shape[1]), F32)
        bcbuf_ref[0:HALO, :] = jnp.zeros((HALO, bcbuf_ref.shape[1]), F32)

    xbuf_ref[HALO:HALO + L, :] = x_ref[...].astype(F32)
    bcbuf_ref[HALO:HALO + L, :] = bc_ref[...].astype(F32)
    xa = _causal_conv(xbuf_ref, cwx_ref[...], L, SSM_CONV) + cbx_ref[...]
    xc_ref[...] = xa * _sigmoid(xa)
    ba = _causal_conv(bcbuf_ref, cwbc_ref[...], L, SSM_CONV) + cbbc_ref[...]
    bcc_ref[...] = (ba * _sigmoid(ba)).astype(BF16)
    xbuf_ref[0:HALO, :] = xbuf_ref[L:L + HALO, :]
    bcbuf_ref[0:HALO, :] = bcbuf_ref[L:L + HALO, :]

    dtr = dt_ref[...] + dtb_ref[...]
    dtv = jnp.maximum(dtr, 0.0) + jnp.log(1.0 + jnp.exp(-jnp.abs(dtr)))
    a = -jnp.exp(alog_ref[...])
    da = dtv * a

    row = lax.broadcasted_iota(jnp.int32, (L, L), 0)
    col = lax.broadcasted_iota(jnp.int32, (L, L), 1)
    causal = row >= col
    tril = jnp.where(causal, 1.0, 0.0).astype(BF16)
    acs = jnp.dot(jnp.concatenate([tril, tril, tril], axis=1),
                  jnp.concatenate(_split_bf16(da, 3), axis=0),
                  preferred_element_type=F32)
    acs_t = acs.T
    last = acs[L - 1:L, :]
    dte = jnp.exp(last - acs)
    cdec = jnp.broadcast_to(jnp.exp(last), (2 * SUBLANES, LANES))

    q = jnp.concatenate([dtv, dtv * dte, cdec], axis=0)
    ex_ref[...] = jnp.dot(jnp.concatenate(_split_bf16(q, 2), axis=1), expand_ref[...],
                          preferred_element_type=F32)
    xc = xc_ref[...]
    xdt_ref[...] = (xc * ex_ref[0:L, :]).astype(BF16)
    xw_ref[...] = (xc * ex_ref[L:2 * L, :]).astype(BF16)

    lane = lax.broadcasted_iota(jnp.int32, (L, LANES), 1)
    lo_half = lane < head_dim
    heads_per_tile = LANES // head_dim
    tiles_per_group = gw // LANES
    neg_inf = jnp.float32(-jnp.inf)

    for g in range(n_groups):
        gs = slice(g * gw, (g + 1) * gw)
        bg = bcc_ref[:, g * n_state:(g + 1) * n_state]
        cg = bcc_ref[:, bdim + g * n_state:bdim + (g + 1) * n_state]
        cb = lax.dot_general(cg, bg, (((1,), (1,)), ((), ())), preferred_element_type=F32)
        cg32 = cg.astype(F32)
        prev = state_ref[:, gs]
        prev_bf = prev.astype(BF16)
        y_tiles = []
        for tl in range(tiles_per_group):
            lhs_parts = []
            rhs_parts = []
            xp = xdt_ref[:, g * gw + tl * LANES:g * gw + (tl + 1) * LANES]
            pp = prev_bf[:, tl * LANES:(tl + 1) * LANES]
            for hh in range(heads_per_tile):
                h = g * heads_per_group + tl * heads_per_tile + hh
                acol = acs[:, h:h + 1]
                arow = acs_t[h:h + 1, :]
                decay = jnp.exp(jnp.where(causal, acol - arow, neg_inf))
                lhs_parts.append((cb * decay).astype(BF16))
                lhs_parts.append((cg32 * jnp.exp(acol)).astype(BF16))
                keep = lo_half if hh == 0 else jnp.logical_not(lo_half)
                rhs_parts.append(jnp.where(keep, xp, jnp.zeros_like(xp)))
                rhs_parts.append(jnp.where(keep, pp, jnp.zeros_like(pp)))
            y_tiles.append(jnp.dot(jnp.concatenate(lhs_parts, axis=1),
                                   jnp.concatenate(rhs_parts, axis=0),
                                   preferred_element_type=F32))
        y = jnp.concatenate(y_tiles, axis=1)

        s_new = lax.dot_general(bg, xw_ref[:, gs], (((0,), (0,)), ((), ())),
                                preferred_element_type=F32)
        state_ref[:, gs] = prev * ex_ref[2 * L:2 * L + 1, gs] + s_new

        yv = y + xc_ref[:, gs] * dexp_ref[:, gs]
        zg = z_ref[:, gs].astype(F32)
        yv = yv * (zg * _sigmoid(zg))
        ms = jnp.mean(yv * yv, axis=-1, keepdims=True)
        o_ref[:, gs] = (yv * lax.rsqrt(ms + EPS) * nw_ref[:, gs]).astype(o_ref.dtype)


def _ssd_core(zxbc, dt_raw, conv_w, conv_b, dt_bias, a_log, d_skip, norm_w, *,
              batch, seq, d_inner, n_groups, n_state, head_dim):
    t = zxbc.shape[0]
    n_heads = d_inner // head_dim
    heads_per_group = n_heads // n_groups
    bdim = n_groups * n_state
    L = SSM_CHUNK
    chunks = seq // L
    assert d_inner % (2 * bdim) == 0 and LANES % head_dim == 0 and n_heads <= LANES

    cwx, cwbc = conv_w[:, :d_inner], conv_w[:, d_inner:]
    cbx, cbbc = conv_b[None, :d_inner], conv_b[None, d_inner:]
    pad = LANES - n_heads
    dtb = jnp.pad(dt_bias, (0, pad))[None, :]
    alog = jnp.pad(a_log, (0, pad))[None, :]
    dexp = jnp.repeat(d_skip, head_dim)[None, :]
    nw = norm_w[None, :]
    e = (jnp.arange(LANES)[:, None] == (jnp.arange(d_inner) // head_dim)[None, :]).astype(BF16)
    expand = jnp.concatenate([e, e], axis=0)

    row_map = lambda b, c: (b * chunks + c, 0)
    const = lambda b, c: (0, 0)
    kern = functools.partial(_ssd_core_kernel, n_groups=n_groups, heads_per_group=heads_per_group,
                             head_dim=head_dim, n_state=n_state)
    return pl.pallas_call(
        kern,
        grid=(batch, chunks),
        in_specs=[
            pl.BlockSpec((L, d_inner), row_map),
            pl.BlockSpec((L, d_inner), lambda b, c: (b * chunks + c, 1)),
            pl.BlockSpec((L, 2 * bdim), lambda b, c: (b * chunks + c, d_inner // bdim)),
            pl.BlockSpec((L, LANES), row_map),
            pl.BlockSpec((SSM_CONV, d_inner), const),
            pl.BlockSpec((1, d_inner), const),
            pl.BlockSpec((SSM_CONV, 2 * bdim), const),
            pl.BlockSpec((1, 2 * bdim), const),
            pl.BlockSpec((1, LANES), const),
            pl.BlockSpec((1, LANES), const),
            pl.BlockSpec((1, d_inner), const),
            pl.BlockSpec((1, d_inner), const),
            pl.BlockSpec((2 * LANES, d_inner), const),
        ],
        out_specs=pl.BlockSpec((L, d_inner), row_map),
        out_shape=jax.ShapeDtypeStruct((t, d_inner), BF16),
        scratch_shapes=[
            pltpu.VMEM((n_state, d_inner), F32),
            pltpu.VMEM((HALO + L, d_inner), F32),
            pltpu.VMEM((HALO + L, 2 * bdim), F32),
            pltpu.VMEM((L, d_inner), F32),
            pltpu.VMEM((L, 2 * bdim), BF16),
            pltpu.VMEM((L, d_inner), BF16),
            pltpu.VMEM((L, d_inner), BF16),
            pltpu.VMEM((2 * L + 2 * SUBLANES, d_inner), F32),
        ],
        compiler_params=pltpu.CompilerParams(
            dimension_semantics=("arbitrary", "arbitrary"),
            vmem_limit_bytes=VMEM_LIMIT_BYTES),
        name="ssd_core",
    )(zxbc, zxbc, zxbc, dt_raw, cwx, cbx, cwbc, cbbc, dtb, alog, dexp, nw, expand)


def _proj_residual_kernel(x_ref, y_ref, w_ref, o_ref):
    o_ref[...] = x_ref[...] + jnp.dot(y_ref[...], w_ref[...], preferred_element_type=F32)


def _proj_residual(x, y, w, *, tm, tn):
    t, d = x.shape
    k = y.shape[1]
    return pl.pallas_call(
        _proj_residual_kernel,
        grid=(t // tm, d // tn),
        in_specs=[
            pl.BlockSpec((tm, tn), lambda i, j: (i, j)),
            pl.BlockSpec((tm, k), lambda i, j: (i, 0)),
            pl.BlockSpec((k, tn), lambda i, j: (0, j)),
        ],
        out_specs=pl.BlockSpec((tm, tn), lambda i, j: (i, j)),
        out_shape=jax.ShapeDtypeStruct((t, d), F32),
        compiler_params=pltpu.CompilerParams(
            dimension_semantics=("arbitrary", "arbitrary"),
            vmem_limit_bytes=VMEM_LIMIT_BYTES),
        name="ssd_outproj_residual",
    )(x, y, w)


ROW_TILE = SUBLANES
COL_BLOCK = 2 * LANES


def _mlp_prologue(x_ref, nw_ref, o_ref, h_ref, zero_refs, *, nch, n_pairs):
    s = pl.program_id(0)

    @pl.when(s == 0)
    def _():
        for r in zero_refs:
            r[...] = jnp.zeros(r.shape, r.dtype)
        o_ref[...] = jnp.zeros(o_ref.shape, o_ref.dtype)

    @pl.when(jnp.logical_and(s % nch == 0, s < n_pairs))
    def _():
        _rmsnorm_rows(x_ref, nw_ref, h_ref)

    @pl.when(jnp.logical_and((s - 2) % nch == 0, s >= 2))
    def _():
        o_ref[...] = x_ref[...]


def _mlp_epilogue(o_ref, fnw_ref, *, nch, final_norm):
    if final_norm:
        s = pl.program_id(0)

        @pl.when(jnp.logical_and((s - 2) % nch == nch - 1, s >= 2))
        def _():
            _rmsnorm_rows(o_ref, fnw_ref, o_ref)


def _by_parity(body, set_a, set_b):
    s = pl.program_id(0)

    @pl.when(s % 2 == 0)
    def _():
        body(set_a, set_b)

    @pl.when(s % 2 == 1)
    def _():
        body(set_b, set_a)


def _elementwise_stage_ids(nch, tiles_per_seq):
    pair = jnp.maximum(pl.program_id(0) - 1, 0)
    return pair % nch, (pair // nch) % tiles_per_seq == 0


class _ConvTaps:
    def __init__(self, w_ref, cols, history):
        self.w = [w_ref[ROW_TILE * k:ROW_TILE * (k + 1), cols] for k in range(3)]
        row = lax.broadcasted_iota(jnp.int32, history.shape, 0)
        self.keep = [row >= 1, row >= 2]
        self.rolled = [pltpu.roll(history, 1, 0), pltpu.roll(history, 2, 0)]

    def __call__(self, u):
        rolled = [pltpu.roll(u, 1, 0), pltpu.roll(u, 2, 0)]
        sh1 = jnp.where(self.keep[0], rolled[0], self.rolled[0])
        sh2 = jnp.where(self.keep[1], rolled[1], self.rolled[1])
        self.rolled = rolled
        return self.w[2] * u + self.w[1] * sh1 + self.w[0] * sh2


def _history(halo_ref, chunk, cols, seq_start):
    hist = halo_ref[chunk, :, cols]
    return jnp.where(seq_start, jnp.zeros_like(hist), hist)


def _ffn_elementwise(g_ref, v_ref, act_ref, cwg_ref, cwv_ref, cbg_ref, cbv_ref, ghalo_ref, vhalo_ref,
                     chunk, seq_start):
    tm, tc = g_ref.shape
    for c0 in range(0, tc, min(COL_BLOCK, tc)):
        cols = slice(c0, c0 + min(COL_BLOCK, tc))
        conv_g = _ConvTaps(cwg_ref, cols, _history(ghalo_ref, chunk, cols, seq_start))
        conv_v = _ConvTaps(cwv_ref, cols, _history(vhalo_ref, chunk, cols, seq_start))
        bias_g = cbg_ref[:, cols]
        bias_v = cbv_ref[:, cols]
        for r0 in range(0, tm, 2 * ROW_TILE):
            halves = []
            for r in (r0, r0 + ROW_TILE):
                g = conv_g(g_ref[r:r + ROW_TILE, cols]) + bias_g
                v = conv_v(v_ref[r:r + ROW_TILE, cols]) + bias_v
                halves.append(g * _sigmoid(g) * v)
            act_ref[r0:r0 + 2 * ROW_TILE, cols] = jnp.concatenate(halves, axis=0).astype(BF16)
        ghalo_ref[chunk, :, cols] = g_ref[tm - ROW_TILE:tm, cols]
        vhalo_ref[chunk, :, cols] = v_ref[tm - ROW_TILE:tm, cols]


def _ffn_kernel(x_ref, nw_ref, wg_ref, wv_ref, cwg_ref, cwv_ref, cbg_ref, cbv_ref, w2_ref, fnw_ref,
                o_ref, h_ref, g0_ref, v0_ref, a0_ref, g1_ref, v1_ref, a1_ref, ghalo_ref, vhalo_ref,
                *, nch, n_pairs, tiles_per_seq, final_norm):
    chunk, seq_start = _elementwise_stage_ids(nch, tiles_per_seq)
    _mlp_prologue(x_ref, nw_ref, o_ref, h_ref,
                  (g0_ref, v0_ref, a0_ref, g1_ref, v1_ref, a1_ref, ghalo_ref, vhalo_ref),
                  nch=nch, n_pairs=n_pairs)

    def body(cur, prv):
        up_g, up_v, act_down = cur
        ew_g, ew_v, act_ew = prv
        _ffn_elementwise(ew_g, ew_v, act_ew, cwg_ref, cwv_ref, cbg_ref, cbv_ref, ghalo_ref, vhalo_ref,
                         chunk, seq_start)
        h = h_ref[...]
        up_g[...] = jnp.dot(h, wg_ref[...], preferred_element_type=F32)
        up_v[...] = jnp.dot(h, wv_ref[...], preferred_element_type=F32)
        o_ref[...] += jnp.dot(act_down[...], w2_ref[...], preferred_element_type=F32)

    _by_parity(body, (g0_ref, v0_ref, a0_ref), (g1_ref, v1_ref, a1_ref))
    _mlp_epilogue(o_ref, fnw_ref, nch=nch, final_norm=final_norm)


def _mlp_index_maps(nch, n_tiles):
    up_tile = lambda s: jnp.minimum(s // nch, n_tiles - 1)
    up_chunk = lambda s: s % nch
    ew_chunk = lambda s: jnp.maximum(s - 1, 0) % nch
    down_chunk = lambda s: jnp.maximum(s - 2, 0) % nch
    down_tile = lambda s: jnp.maximum(s - 2, 0) // nch
    return up_tile, up_chunk, ew_chunk, down_chunk, down_tile


def _tap_rows(conv_w):
    return jnp.repeat(conv_w, ROW_TILE, axis=0)


def _conv_ffn(x, nw, w_up, conv_w, conv_b, w_down, fnw, *, seq, tm, tc, final_norm):
    t, d = x.shape
    d_ff = w_down.shape[0]
    nch = d_ff // tc
    n_tiles = t // tm
    assert nch >= 3 and tc % min(COL_BLOCK, tc) == 0 and tm % (2 * ROW_TILE) == 0
    up_tile, up_chunk, ew_chunk, down_chunk, down_tile = _mlp_index_maps(nch, n_tiles)
    kern = functools.partial(_ffn_kernel, nch=nch, n_pairs=n_tiles * nch, tiles_per_seq=seq // tm,
                             final_norm=final_norm)
    taps = _tap_rows(conv_w)
    bias = jnp.broadcast_to(conv_b, (ROW_TILE, conv_b.shape[1]))
    n_tap_rows = FFN_CONV * ROW_TILE
    return pl.pallas_call(
        kern,
        grid=(n_tiles * nch + 2,),
        in_specs=[
            pl.BlockSpec((tm, d), lambda s: (up_tile(s), 0)),
            pl.BlockSpec((1, d), lambda s: (0, 0)),
            pl.BlockSpec((d, tc), lambda s: (0, up_chunk(s))),
            pl.BlockSpec((d, tc), lambda s: (0, nch + up_chunk(s))),
            pl.BlockSpec((n_tap_rows, tc), lambda s: (0, ew_chunk(s))),
            pl.BlockSpec((n_tap_rows, tc), lambda s: (0, nch + ew_chunk(s))),
            pl.BlockSpec((ROW_TILE, tc), lambda s: (0, ew_chunk(s))),
            pl.BlockSpec((ROW_TILE, tc), lambda s: (0, nch + ew_chunk(s))),
            pl.BlockSpec((tc, d), lambda s: (down_chunk(s), 0)),
            pl.BlockSpec((1, d), lambda s: (0, 0)),
        ],
        out_specs=pl.BlockSpec((tm, d), lambda s: (down_tile(s), 0)),
        out_shape=jax.ShapeDtypeStruct((t, d), F32),
        scratch_shapes=[
            pltpu.VMEM((tm, d), BF16),
            pltpu.VMEM((tm, tc), F32), pltpu.VMEM((tm, tc), F32), pltpu.VMEM((tm, tc), BF16),
            pltpu.VMEM((tm, tc), F32), pltpu.VMEM((tm, tc), F32), pltpu.VMEM((tm, tc), BF16),
            pltpu.VMEM((nch, ROW_TILE, tc), F32),
            pltpu.VMEM((nch, ROW_TILE, tc), F32),
        ],
        compiler_params=pltpu.CompilerParams(
            dimension_semantics=("arbitrary",),
            vmem_limit_bytes=VMEM_LIMIT_BYTES),
        name="conv_ffn",
    )(x, nw, w_up, w_up, taps, taps, bias, bias, w_down, fnw)


def _sc_elementwise(b_ref, c_ref, hid_ref, act_ref, cw_ref, uhalo_ref, chunk, seq_start):
    tm, tc = b_ref.shape
    for c0 in range(0, tc, min(COL_BLOCK, tc)):
        cols = slice(c0, c0 + min(COL_BLOCK, tc))
        conv = _ConvTaps(cw_ref, cols, _history(uhalo_ref, chunk, cols, seq_start))
        u = None
        for r0 in range(0, tm, 2 * ROW_TILE):
            halves = []
            for r in (r0, r0 + ROW_TILE):
                u = c_ref[r:r + ROW_TILE, cols] * hid_ref[r:r + ROW_TILE, cols]
                halves.append(b_ref[r:r + ROW_TILE, cols] * conv(u))
            act_ref[r0:r0 + 2 * ROW_TILE, cols] = jnp.concatenate(halves, axis=0).astype(BF16)
        uhalo_ref[chunk, :, cols] = u


def _sc_kernel(x_ref, nw_ref, wb_ref, wc_ref, wh_ref, cw_ref, w2_ref,
               o_ref, h_ref, b0_ref, c0_ref, hid0_ref, a0_ref, b1_ref, c1_ref, hid1_ref, a1_ref, uhalo_ref,
               *, nch, n_pairs, tiles_per_seq):
    chunk, seq_start = _elementwise_stage_ids(nch, tiles_per_seq)
    _mlp_prologue(x_ref, nw_ref, o_ref, h_ref,
                  (b0_ref, c0_ref, hid0_ref, a0_ref, b1_ref, c1_ref, hid1_ref, a1_ref, uhalo_ref),
                  nch=nch, n_pairs=n_pairs)

    def body(cur, prv):
        up_b, up_c, up_hid, act_down = cur
        ew_b, ew_c, ew_hid, act_ew = prv
        _sc_elementwise(ew_b, ew_c, ew_hid, act_ew, cw_ref, uhalo_ref, chunk, seq_start)
        h = h_ref[...]
        up_b[...] = jnp.dot(h, wb_ref[...], preferred_element_type=F32)
        up_c[...] = jnp.dot(h, wc_ref[...], preferred_element_type=F32)
        up_hid[...] = jnp.dot(h, wh_ref[...], preferred_element_type=F32)
        o_ref[...] += jnp.dot(act_down[...], w2_ref[...], preferred_element_type=F32)

    _by_parity(body, (b0_ref, c0_ref, hid0_ref, a0_ref), (b1_ref, c1_ref, hid1_ref, a1_ref))


def _short_conv(x, nw, w_in, conv_w, w_out, *, seq, tm, tc):
    t, d = x.shape
    sc_dim = w_out.shape[0]
    nch = sc_dim // tc
    n_tiles = t // tm
    assert nch >= 3 and tc % min(COL_BLOCK, tc) == 0 and tm % (2 * ROW_TILE) == 0
    up_tile, up_chunk, ew_chunk, down_chunk, down_tile = _mlp_index_maps(nch, n_tiles)
    kern = functools.partial(_sc_kernel, nch=nch, n_pairs=n_tiles * nch, tiles_per_seq=seq // tm)
    up_buf = pltpu.VMEM((tm, tc), F32)
    return pl.pallas_call(
        kern,
        grid=(n_tiles * nch + 2,),
        in_specs=[
            pl.BlockSpec((tm, d), lambda s: (up_tile(s), 0)),
            pl.BlockSpec((1, d), lambda s: (0, 0)),
            pl.BlockSpec((d, tc), lambda s: (0, up_chunk(s))),
            pl.BlockSpec((d, tc), lambda s: (0, nch + up_chunk(s))),
            pl.BlockSpec((d, tc), lambda s: (0, 2 * nch + up_chunk(s))),
            pl.BlockSpec((SC_WIDTH * ROW_TILE, tc), lambda s: (0, ew_chunk(s))),
            pl.BlockSpec((tc, d), lambda s: (down_chunk(s), 0)),
        ],
        out_specs=pl.BlockSpec((tm, d), lambda s: (down_tile(s), 0)),
        out_shape=jax.ShapeDtypeStruct((t, d), F32),
        scratch_shapes=[
            pltpu.VMEM((tm, d), BF16),
            up_buf, up_buf, up_buf, pltpu.VMEM((tm, tc), BF16),
            up_buf, up_buf, up_buf, pltpu.VMEM((tm, tc), BF16),
            pltpu.VMEM((nch, ROW_TILE, tc), F32),
        ],
        compiler_params=pltpu.CompilerParams(
            dimension_semantics=("arbitrary",),
            vmem_limit_bytes=VMEM_LIMIT_BYTES),
        name="short_conv_mixer",
    )(x, nw, w_in, w_in, w_in, _tap_rows(conv_w), w_out)


def _tile(n, pref):
    c = min(pref, n)
    while n % c:
        c -= LANES
    return c


def _trunk(x, mix_norm_w, ffn_norm_w, final_norm_w,
           ssd_w_in, ssd_conv_w, ssd_conv_b, ssd_dt_bias, ssd_a_log, ssd_d, ssd_norm_w, ssd_w_out,
           sc_w_in, sc_conv_w, sc_w_out,
           ffn_w_up, ffn_conv_w, ffn_conv_b, ffn_w_down,
           *, head_dim=SSM_HEAD_DIM, n_groups=SSM_GROUPS, n_state=SSM_STATE,
           tm_proj=1024, tn_proj=1024, tm_mlp=512, tc_mlp=512):
    batch, seq, d_model = x.shape
    depth = mix_norm_w.shape[0]
    t = batch * seq
    d_inner = ssd_w_out.shape[1]
    n_heads = ssd_dt_bias.shape[1]
    main_cols = ssd_w_in.shape[2] - n_heads
    assert d_inner == n_heads * head_dim

    tm_proj = _tile(seq, tm_proj)
    tm_mlp = _tile(seq, tm_mlp)
    xs = x.reshape(t, d_model)
    for i in range(depth):
        j = i // 2
        nw = mix_norm_w[i][None, :]
        if i % 2 == 0:
            w_main = ssd_w_in[j, :, :main_cols].astype(BF16)
            w_dt = jnp.pad(ssd_w_in[j, :, main_cols:], ((0, 0), (0, LANES - n_heads))).astype(BF16)
            zxbc, dt_raw = _norm_inproj(xs, nw, w_main, w_dt, tm=tm_proj, tn=_tile(main_cols, tn_proj))
            y = _ssd_core(zxbc, dt_raw, ssd_conv_w[j], ssd_conv_b[j], ssd_dt_bias[j], ssd_a_log[j],
                          ssd_d[j], ssd_norm_w[j], batch=batch, seq=seq, d_inner=d_inner,
                          n_groups=n_groups, n_state=n_state, head_dim=head_dim)
            xs = _proj_residual(xs, y, ssd_w_out[j].astype(BF16), tm=tm_mlp, tn=_tile(d_model, tn_proj))
        else:
            xs = _short_conv(xs, nw, sc_w_in[j].astype(BF16), sc_conv_w[j], sc_w_out[j].astype(BF16),
                             seq=seq, tm=tm_mlp, tc=_tile(sc_w_out.shape[1], tc_mlp))
        xs = _conv_ffn(xs, ffn_norm_w[i][None, :], ffn_w_up[i].astype(BF16), ffn_conv_w[i],
                       ffn_conv_b[i][None, :], ffn_w_down[i].astype(BF16), final_norm_w[None, :],
                       seq=seq, tm=tm_mlp, tc=_tile(ffn_w_down.shape[1], tc_mlp),
                       final_norm=(i == depth - 1))
    return xs.reshape(batch, seq, d_model)


def kernel(x, mix_norm_w, ffn_norm_w, final_norm_w, ssd_w_in, ssd_conv_w, ssd_conv_b, ssd_dt_bias,
           ssd_a_log, ssd_d, ssd_norm_w, ssd_w_out, sc_w_in, sc_conv_w, sc_w_out,
           ffn_w_up, ffn_conv_w, ffn_conv_b, ffn_w_down):
    return _trunk(x, mix_norm_w, ffn_norm_w, final_norm_w,
                  ssd_w_in, ssd_conv_w, ssd_conv_b, ssd_dt_bias, ssd_a_log, ssd_d, ssd_norm_w, ssd_w_out,
                  sc_w_in, sc_conv_w, sc_w_out,
                  ffn_w_up, ffn_conv_w, ffn_conv_b, ffn_w_down)
```

```python
import functools

import jax
import jax.numpy as jnp
from jax import lax
from jax.experimental import pallas as pl
from jax.experimental.pallas import tpu as pltpu

F32 = jnp.float32
BF16 = jnp.bfloat16

EPS = 1e-5
SSM_HEAD_DIM = 64
SSM_GROUPS = 8
SSM_STATE = 128
SSM_CHUNK = 128
SSM_CONV = 4
SC_WIDTH = 3
FFN_CONV = 3

LANES = 128
SUBLANES = 8
VMEM_LIMIT_BYTES = 56 * 1024 * 1024

HALO = SUBLANES


def _sigmoid(v):
    return 1.0 / (1.0 + jnp.exp(-v))


def _rmsnorm_rows(x_ref, w_ref, out_ref, rows_per_step=64):
    n_rows = x_ref.shape[0]
    w = w_ref[...]

    def body(r, carry):
        sl = pl.ds(pl.multiple_of(r * rows_per_step, rows_per_step), rows_per_step)
        x = x_ref[sl, :]
        ms = jnp.mean(x * x, axis=-1, keepdims=True)
        out_ref[sl, :] = (x * lax.rsqrt(ms + EPS) * w).astype(out_ref.dtype)
        return carry

    lax.fori_loop(0, n_rows // rows_per_step, body, 0)


def _causal_conv(buf_ref, w, n_rows, width):
    acc = None
    for k in range(width):
        start = HALO - (width - 1) + k
        term = w[k:k + 1, :] * buf_ref[start:start + n_rows, :]
        acc = term if acc is None else acc + term
    return acc


def _col_blocks(w, width):
    k, n = w.shape
    return w.reshape(k, n // width, width).transpose(1, 0, 2)


def _norm_inproj_kernel(x_ref, nw_ref, w_ref, wdt_ref, o_ref, dt_ref, h_ref):
    @pl.when(pl.program_id(1) == 0)
    def _():
        _rmsnorm_rows(x_ref, nw_ref, h_ref)
        dt_ref[...] = jnp.dot(h_ref[...], wdt_ref[...], preferred_element_type=F32)

    o_ref[...] = jnp.dot(h_ref[...], w_ref[...], preferred_element_type=F32).astype(o_ref.dtype)


def _norm_inproj(x, nw, w, wdt, *, tm, tn):
    t, d = x.shape
    n = w.shape[1]
    return pl.pallas_call(
        _norm_inproj_kernel,
        grid=(t // tm, n // tn),
        in_specs=[
            pl.BlockSpec((tm, d), lambda i, j: (i, 0)),
            pl.BlockSpec((1, d), lambda i, j: (0, 0)),
            pl.BlockSpec((None, d, tn), lambda i, j: (j, 0, 0)),
            pl.BlockSpec((d, LANES), lambda i, j: (0, 0)),
        ],
        out_specs=[
            pl.BlockSpec((tm, tn), lambda i, j: (i, j)),
            pl.BlockSpec((tm, LANES), lambda i, j: (i, 0)),
        ],
        out_shape=[
            jax.ShapeDtypeStruct((t, n), BF16),
            jax.ShapeDtypeStruct((t, LANES), F32),
        ],
        scratch_shapes=[pltpu.VMEM((tm, d), BF16)],
        compiler_params=pltpu.CompilerParams(
            dimension_semantics=("arbitrary", "arbitrary"),
            vmem_limit_bytes=VMEM_LIMIT_BYTES),
        name="ssd_norm_inproj",
    )(x, nw, _col_blocks(w, tn), wdt)


def _split_bf16(v, parts):
    out = []
    rem = v
    for _ in range(parts):
        p = rem.astype(BF16)
        out.append(p)
        rem = rem - p.astype(F32)
    return out


def _ssd_core_kernel(z_ref, x_ref, bc_ref, dt_ref, cwx_ref, cbx_ref, cwbc_ref, cbbc_ref,
                     dtb_ref, alog_ref, dexp_ref, nw_ref, expand_ref, o_ref,
                     state_ref, xbuf_ref, bcbuf_ref, xc_ref, bcc_ref, xdt_ref, xw_ref, ex_ref,
                     *, n_groups, heads_per_group, head_dim, n_state):
    L = SSM_CHUNK
    gw = heads_per_group * head_dim
    bdim = n_groups * n_state

    @pl.when(pl.program_id(1) == 0)
    def _():
        state_ref[...] = jnp.zeros_like(state_ref)
        xbuf_ref[0:HALO, :] = jnp.zeros((HALO, xbuf_ref.shape[1]), F32)
        bcbuf_ref[0:HALO, :] = jnp.zeros((HALO, bcbuf_ref.shape[1]), F32)

    xbuf_ref[HALO:HALO + L, :] = x_ref[...].astype(F32)
    bcbuf_ref[HALO:HALO + L, :] = bc_ref[...].astype(F32)
    xa = _causal_conv(xbuf_ref, cwx_ref[...], L, SSM_CONV) + cbx_ref[...]
    xc_ref[...] = xa * _sigmoid(xa)
    ba = _causal_conv(bcbuf_ref, cwbc_ref[...], L, SSM_CONV) + cbbc_ref[...]
    bcc_ref[...] = (ba * _sigmoid(ba)).astype(BF16)
    xbuf_ref[0:HALO, :] = xbuf_ref[L:L + HALO, :]
    bcbuf_ref[0:HALO, :] = bcbuf_ref[L:L + HALO, :]

    dtr = dt_ref[...] + dtb_ref[...]
    dtv = jnp.maximum(dtr, 0.0) + jnp.log(1.0 + jnp.exp(-jnp.abs(dtr)))
    a = -jnp.exp(alog_ref[...])
    da = dtv * a

    row = lax.broadcasted_iota(jnp.int32, (L, L), 0)
    col = lax.broadcasted_iota(jnp.int32, (L, L), 1)
    causal = row >= col
    tril = jnp.where(causal, 1.0, 0.0).astype(BF16)
    acs = jnp.dot(jnp.concatenate([tril, tril, tril], axis=1),
                  jnp.concatenate(_split_bf16(da, 3), axis=0),
                  preferred_element_type=F32)
    acs_t = acs.T
    last = acs[L - 1:L, :]
    dte = jnp.exp(last - acs)
    cdec = jnp.broadcast_to(jnp.exp(last), (2 * SUBLANES, LANES))

    q = jnp.concatenate([dtv, dtv * dte, cdec], axis=0)
    ex_ref[...] = jnp.dot(jnp.concatenate(_split_bf16(q, 2), axis=1), expand_ref[...],
                          preferred_element_type=F32)
    xc = xc_ref[...]
    xdt_ref[...] = (xc * ex_ref[0:L, :]).astype(BF16)
    xw_ref[...] = (xc * ex_ref[L:2 * L, :]).astype(BF16)

    lane = lax.broadcasted_iota(jnp.int32, (L, LANES), 1)
    lo_half = lane < head_dim
    heads_per_tile = LANES // head_dim
    tiles_per_group = gw // LANES
    neg_inf = jnp.float32(-jnp.inf)

    for g in range(n_groups):
        gs = slice(g * gw, (g + 1) * gw)
        bg = bcc_ref[:, g * n_state:(g + 1) * n_state]
        cg = bcc_ref[:, bdim + g * n_state:bdim + (g + 1) * n_state]
        cb = lax.dot_general(cg, bg, (((1,), (1,)), ((), ())), preferred_element_type=F32)
        cg32 = cg.astype(F32)
        prev = state_ref[:, gs]
        prev_bf = prev.astype(BF16)
        y_tiles = []
        for tl in range(tiles_per_group):
            lhs_parts = []
            rhs_parts = []
            xp = xdt_ref[:, g * gw + tl * LANES:g * gw + (tl + 1) * LANES]
            pp = prev_bf[:, tl * LANES:(tl + 1) * LANES]
            for hh in range(heads_per_tile):
                h = g * heads_per_group + tl * heads_per_tile + hh
                acol = acs[:, h:h + 1]
                arow = acs_t[h:h + 1, :]
                decay = jnp.exp(jnp.where(causal, acol - arow, neg_inf))
                lhs_parts.append((cb * decay).astype(BF16))
                lhs_parts.append((cg32 * jnp.exp(acol)).astype(BF16))
                keep = lo_half if hh == 0 else jnp.logical_not(lo_half)
                rhs_parts.append(jnp.where(keep, xp, jnp.zeros_like(xp)))
                rhs_parts.append(jnp.where(keep, pp, jnp.zeros_like(pp)))
            y_tiles.append(jnp.dot(jnp.concatenate(lhs_parts, axis=1),
                                   jnp.concatenate(rhs_parts, axis=0),
                                   preferred_element_type=F32))
        y = jnp.concatenate(y_tiles, axis=1)

        s_new = lax.dot_general(bg, xw_ref[:, gs], (((0,), (0,)), ((), ())),
                                preferred_element_type=F32)
        state_ref[:, gs] = prev * ex_ref[2 * L:2 * L + 1, gs] + s_new

        yv = y + xc_ref[:, gs] * dexp_ref[:, gs]
        zg = z_ref[:, gs].astype(F32)
        yv = yv * (zg * _sigmoid(zg))
        ms = jnp.mean(yv * yv, axis=-1, keepdims=True)
        o_ref[:, gs] = (yv * lax.rsqrt(ms + EPS) * nw_ref[:, gs]).astype(o_ref.dtype)


def _ssd_core(zxbc, dt_raw, conv_w, conv_b, dt_bias, a_log, d_skip, norm_w, *,
              batch, seq, d_inner, n_groups, n_state, head_dim):
    t = zxbc.shape[0]
    n_heads = d_inner // head_dim
    heads_per_group = n_heads // n_groups
    bdim = n_groups * n_state
    L = SSM_CHUNK
    chunks = seq // L
    assert d_inner % (2 * bdim) == 0 and LANES % head_dim == 0 and n_heads <= LANES

    cwx, cwbc = conv_w[:, :d_inner], conv_w[:, d_inner:]
    cbx, cbbc = conv_b[None, :d_inner], conv_b[None, d_inner:]
    pad = LANES - n_heads
    dtb = jnp.pad(dt_bias, (0, pad))[None, :]
    alog = jnp.pad(a_log, (0, pad))[None, :]
    dexp = jnp.repeat(d_skip, head_dim)[None, :]
    nw = norm_w[None, :]
    e = (jnp.arange(LANES)[:, None] == (jnp.arange(d_inner) // head_dim)[None, :]).astype(BF16)
    expand = jnp.concatenate([e, e], axis=0)

    row_map = lambda b, c: (b * chunks + c, 0)
    const = lambda b, c: (0, 0)
    kern = functools.partial(_ssd_core_kernel, n_groups=n_groups, heads_per_group=heads_per_group,
                             head_dim=head_dim, n_state=n_state)
    return pl.pallas_call(
        kern,
        grid=(batch, chunks),
        in_specs=[
            pl.BlockSpec((L, d_inner), row_map),
            pl.BlockSpec((L, d_inner), lambda b, c: (b * chunks + c, 1)),
            pl.BlockSpec((L, 2 * bdim), lambda b, c: (b * chunks + c, d_inner // bdim)),
            pl.BlockSpec((L, LANES), row_map),
            pl.BlockSpec((SSM_CONV, d_inner), const),
            pl.BlockSpec((1, d_inner), const),
            pl.BlockSpec((SSM_CONV, 2 * bdim), const),
            pl.BlockSpec((1, 2 * bdim), const),
            pl.BlockSpec((1, LANES), const),
            pl.BlockSpec((1, LANES), const),
            pl.BlockSpec((1, d_inner), const),
            pl.BlockSpec((1, d_inner), const),
            pl.BlockSpec((2 * LANES, d_inner), const),
        ],
        out_specs=pl.BlockSpec((L, d_inner), row_map),
        out_shape=jax.ShapeDtypeStruct((t, d_inner), BF16),
        scratch_shapes=[
            pltpu.VMEM((n_state, d_inner), F32),
            pltpu.VMEM((HALO + L, d_inner), F32),
            pltpu.VMEM((HALO + L, 2 * bdim), F32),
            pltpu.VMEM((L, d_inner), F32),
            pltpu.VMEM((L, 2 * bdim), BF16),
            pltpu.VMEM((L, d_inner), BF16),
            pltpu.VMEM((L, d_inner), BF16),
            pltpu.VMEM((2 * L + 2 * SUBLANES, d_inner), F32),
        ],
        compiler_params=pltpu.CompilerParams(
            dimension_semantics=("arbitrary", "arbitrary"),
            vmem_limit_bytes=VMEM_LIMIT_BYTES),
        name="ssd_core",
    )(zxbc, zxbc, zxbc, dt_raw, cwx, cbx, cwbc, cbbc, dtb, alog, dexp, nw, expand)


def _proj_residual_kernel(x_ref, y_ref, w_ref, o_ref):
    o_ref[...] = x_ref[...] + jnp.dot(y_ref[...], w_ref[...], preferred_element_type=F32)


def _proj_residual(x, y, w, *, tm, tn):
    t, d = x.shape
    k = y.shape[1]
    return pl.pallas_call(
        _proj_residual_kernel,
        grid=(t // tm, d // tn),
        in_specs=[
            pl.BlockSpec((tm, tn), lambda i, j: (i, j)),
            pl.BlockSpec((tm, k), lambda i, j: (i, 0)),
            pl.BlockSpec((None, k, tn), lambda i, j: (j, 0, 0)),
        ],
        out_specs=pl.BlockSpec((tm, tn), lambda i, j: (i, j)),
        out_shape=jax.ShapeDtypeStruct((t, d), F32),
        compiler_params=pltpu.CompilerParams(
            dimension_semantics=("arbitrary", "arbitrary"),
            vmem_limit_bytes=VMEM_LIMIT_BYTES),
        name="ssd_outproj_residual",
    )(x, y, _col_blocks(w, tn))


ROW_TILE = SUBLANES
COL_BLOCK = 2 * LANES


def _mlp_prologue(x_ref, nw_ref, o_ref, h_ref, zero_refs, *, nch, n_pairs):
    s = pl.program_id(0)

    @pl.when(s == 0)
    def _():
        for r in zero_refs:
            r[...] = jnp.zeros(r.shape, r.dtype)
        o_ref[...] = jnp.zeros(o_ref.shape, o_ref.dtype)

    @pl.when(jnp.logical_and(s % nch == 0, s < n_pairs))
    def _():
        _rmsnorm_rows(x_ref, nw_ref, h_ref)

    @pl.when(jnp.logical_and((s - 2) % nch == 0, s >= 2))
    def _():
        o_ref[...] = x_ref[...]


def _mlp_epilogue(o_ref, fnw_ref, *, nch, final_norm):
    if final_norm:
        s = pl.program_id(0)

        @pl.when(jnp.logical_and((s - 2) % nch == nch - 1, s >= 2))
        def _():
            _rmsnorm_rows(o_ref, fnw_ref, o_ref)


def _by_parity(body, set_a, set_b):
    s = pl.program_id(0)

    @pl.when(s % 2 == 0)
    def _():
        body(set_a, set_b)

    @pl.when(s % 2 == 1)
    def _():
        body(set_b, set_a)


def _elementwise_stage_ids(nch, tiles_per_seq):
    pair = jnp.maximum(pl.program_id(0) - 1, 0)
    return pair % nch, (pair // nch) % tiles_per_seq == 0


class _ConvTaps:
    def __init__(self, w_ref, cols, history):
        self.w = [w_ref[ROW_TILE * k:ROW_TILE * (k + 1), cols] for k in range(3)]
        row = lax.broadcasted_iota(jnp.int32, history.shape, 0)
        self.keep = [row >= 1, row >= 2]
        self.rolled = [pltpu.roll(history, 1, 0), pltpu.roll(history, 2, 0)]

    def __call__(self, u):
        rolled = [pltpu.roll(u, 1, 0), pltpu.roll(u, 2, 0)]
        sh1 = jnp.where(self.keep[0], rolled[0], self.rolled[0])
        sh2 = jnp.where(self.keep[1], rolled[1], self.rolled[1])
        self.rolled = rolled
        return self.w[2] * u + self.w[1] * sh1 + self.w[0] * sh2


def _history(halo_ref, chunk, cols, seq_start):
    hist = halo_ref[chunk, :, cols]
    return jnp.where(seq_start, jnp.zeros_like(hist), hist)


def _ffn_elementwise(g_ref, v_ref, act_ref, cwg_ref, cwv_ref, cbg_ref, cbv_ref, ghalo_ref, vhalo_ref,
                     chunk, seq_start):
    tm, tc = g_ref.shape
    for c0 in range(0, tc, min(COL_BLOCK, tc)):
        cols = slice(c0, c0 + min(COL_BLOCK, tc))
        conv_g = _ConvTaps(cwg_ref, cols, _history(ghalo_ref, chunk, cols, seq_start))
        conv_v = _ConvTaps(cwv_ref, cols, _history(vhalo_ref, chunk, cols, seq_start))
        bias_g = cbg_ref[:, cols]
        bias_v = cbv_ref[:, cols]
        for r0 in range(0, tm, 2 * ROW_TILE):
            halves = []
            for r in (r0, r0 + ROW_TILE):
                g = conv_g(g_ref[r:r + ROW_TILE, cols]) + bias_g
                v = conv_v(v_ref[r:r + ROW_TILE, cols]) + bias_v
                halves.append(g * _sigmoid(g) * v)
            act_ref[r0:r0 + 2 * ROW_TILE, cols] = jnp.concatenate(halves, axis=0).astype(BF16)
        ghalo_ref[chunk, :, cols] = g_ref[tm - ROW_TILE:tm, cols]
        vhalo_ref[chunk, :, cols] = v_ref[tm - ROW_TILE:tm, cols]


def _ffn_kernel(x_ref, nw_ref, wg_ref, wv_ref, cwg_ref, cwv_ref, cbg_ref, cbv_ref, w2_ref, fnw_ref,
                o_ref, h_ref, g0_ref, v0_ref, a0_ref, g1_ref, v1_ref, a1_ref, ghalo_ref, vhalo_ref,
                *, nch, n_pairs, tiles_per_seq, final_norm):
    chunk, seq_start = _elementwise_stage_ids(nch, tiles_per_seq)
    _mlp_prologue(x_ref, nw_ref, o_ref, h_ref,
                  (g0_ref, v0_ref, a0_ref, g1_ref, v1_ref, a1_ref, ghalo_ref, vhalo_ref),
                  nch=nch, n_pairs=n_pairs)

    def body(cur, prv):
        up_g, up_v, act_down = cur
        ew_g, ew_v, act_ew = prv
        _ffn_elementwise(ew_g, ew_v, act_ew, cwg_ref, cwv_ref, cbg_ref, cbv_ref, ghalo_ref, vhalo_ref,
                         chunk, seq_start)
        h = h_ref[...]
        up_g[...] = jnp.dot(h, wg_ref[...], preferred_element_type=F32)
        up_v[...] = jnp.dot(h, wv_ref[...], preferred_element_type=F32)
        o_ref[...] += jnp.dot(act_down[...], w2_ref[...], preferred_element_type=F32)

    _by_parity(body, (g0_ref, v0_ref, a0_ref), (g1_ref, v1_ref, a1_ref))
    _mlp_epilogue(o_ref, fnw_ref, nch=nch, final_norm=final_norm)


def _mlp_index_maps(nch, n_tiles):
    up_tile = lambda s: jnp.minimum(s // nch, n_tiles - 1)
    up_chunk = lambda s: s % nch
    ew_chunk = lambda s: jnp.maximum(s - 1, 0) % nch
    down_chunk = lambda s: jnp.maximum(s - 2, 0) % nch
    down_tile = lambda s: jnp.maximum(s - 2, 0) // nch
    return up_tile, up_chunk, ew_chunk, down_chunk, down_tile


def _tap_rows(conv_w):
    return jnp.repeat(conv_w, ROW_TILE, axis=0)


def _conv_ffn(x, nw, w_up, conv_w, conv_b, w_down, fnw, *, seq, tm, tc, final_norm):
    t, d = x.shape
    d_ff = w_down.shape[0]
    nch = d_ff // tc
    n_tiles = t // tm
    assert nch >= 3 and tc % min(COL_BLOCK, tc) == 0 and tm % (2 * ROW_TILE) == 0
    up_tile, up_chunk, ew_chunk, down_chunk, down_tile = _mlp_index_maps(nch, n_tiles)
    kern = functools.partial(_ffn_kernel, nch=nch, n_pairs=n_tiles * nch, tiles_per_seq=seq // tm,
                             final_norm=final_norm)
    w_up_blocks = _col_blocks(w_up, tc)
    taps = _tap_rows(conv_w)
    bias = jnp.broadcast_to(conv_b, (ROW_TILE, conv_b.shape[1]))
    n_tap_rows = FFN_CONV * ROW_TILE
    return pl.pallas_call(
        kern,
        grid=(n_tiles * nch + 2,),
        in_specs=[
            pl.BlockSpec((tm, d), lambda s: (up_tile(s), 0)),
            pl.BlockSpec((1, d), lambda s: (0, 0)),
            pl.BlockSpec((None, d, tc), lambda s: (up_chunk(s), 0, 0)),
            pl.BlockSpec((None, d, tc), lambda s: (nch + up_chunk(s), 0, 0)),
            pl.BlockSpec((n_tap_rows, tc), lambda s: (0, ew_chunk(s))),
            pl.BlockSpec((n_tap_rows, tc), lambda s: (0, nch + ew_chunk(s))),
            pl.BlockSpec((ROW_TILE, tc), lambda s: (0, ew_chunk(s))),
            pl.BlockSpec((ROW_TILE, tc), lambda s: (0, nch + ew_chunk(s))),
            pl.BlockSpec((tc, d), lambda s: (down_chunk(s), 0)),
            pl.BlockSpec((1, d), lambda s: (0, 0)),
        ],
        out_specs=pl.BlockSpec((tm, d), lambda s: (down_tile(s), 0)),
        out_shape=jax.ShapeDtypeStruct((t, d), F32),
        scratch_shapes=[
            pltpu.VMEM((tm, d), BF16),
            pltpu.VMEM((tm, tc), F32), pltpu.VMEM((tm, tc), F32), pltpu.VMEM((tm, tc), BF16),
            pltpu.VMEM((tm, tc), F32), pltpu.VMEM((tm, tc), F32), pltpu.VMEM((tm, tc), BF16),
            pltpu.VMEM((nch, ROW_TILE, tc), F32),
            pltpu.VMEM((nch, ROW_TILE, tc), F32),
        ],
        compiler_params=pltpu.CompilerParams(
            dimension_semantics=("arbitrary",),
            vmem_limit_bytes=VMEM_LIMIT_BYTES),
        name="conv_ffn",
    )(x, nw, w_up_blocks, w_up_blocks, taps, taps, bias, bias, w_down, fnw)


def _sc_elementwise(b_ref, c_ref, hid_ref, act_ref, cw_ref, uhalo_ref, chunk, seq_start):
    tm, tc = b_ref.shape
    for c0 in range(0, tc, min(COL_BLOCK, tc)):
        cols = slice(c0, c0 + min(COL_BLOCK, tc))
        conv = _ConvTaps(cw_ref, cols, _history(uhalo_ref, chunk, cols, seq_start))
        u = None
        for r0 in range(0, tm, 2 * ROW_TILE):
            halves = []
            for r in (r0, r0 + ROW_TILE):
                u = c_ref[r:r + ROW_TILE, cols] * hid_ref[r:r + ROW_TILE, cols]
                halves.append(b_ref[r:r + ROW_TILE, cols] * conv(u))
            act_ref[r0:r0 + 2 * ROW_TILE, cols] = jnp.concatenate(halves, axis=0).astype(BF16)
        uhalo_ref[chunk, :, cols] = u


def _sc_kernel(x_ref, nw_ref, wb_ref, wc_ref, wh_ref, cw_ref, w2_ref,
               o_ref, h_ref, b0_ref, c0_ref, hid0_ref, a0_ref, b1_ref, c1_ref, hid1_ref, a1_ref, uhalo_ref,
               *, nch, n_pairs, tiles_per_seq):
    chunk, seq_start = _elementwise_stage_ids(nch, tiles_per_seq)
    _mlp_prologue(x_ref, nw_ref, o_ref, h_ref,
                  (b0_ref, c0_ref, hid0_ref, a0_ref, b1_ref, c1_ref, hid1_ref, a1_ref, uhalo_ref),
                  nch=nch, n_pairs=n_pairs)

    def body(cur, prv):
        up_b, up_c, up_hid, act_down = cur
        ew_b, ew_c, ew_hid, act_ew = prv
        _sc_elementwise(ew_b, ew_c, ew_hid, act_ew, cw_ref, uhalo_ref, chunk, seq_start)
        h = h_ref[...]
        up_b[...] = jnp.dot(h, wb_ref[...], preferred_element_type=F32)
        up_c[...] = jnp.dot(h, wc_ref[...], preferred_element_type=F32)
        up_hid[...] = jnp.dot(h, wh_ref[...], preferred_element_type=F32)
        o_ref[...] += jnp.dot(act_down[...], w2_ref[...], preferred_element_type=F32)

    _by_parity(body, (b0_ref, c0_ref, hid0_ref, a0_ref), (b1_ref, c1_ref, hid1_ref, a1_ref))


def _short_conv(x, nw, w_in, conv_w, w_out, *, seq, tm, tc):
    t, d = x.shape
    sc_dim = w_out.shape[0]
    nch = sc_dim // tc
    n_tiles = t // tm
    assert nch >= 3 and tc % min(COL_BLOCK, tc) == 0 and tm % (2 * ROW_TILE) == 0
    up_tile, up_chunk, ew_chunk, down_chunk, down_tile = _mlp_index_maps(nch, n_tiles)
    kern = functools.partial(_sc_kernel, nch=nch, n_pairs=n_tiles * nch, tiles_per_seq=seq // tm)
    w_in_blocks = _col_blocks(w_in, tc)
    up_buf = pltpu.VMEM((tm, tc), F32)
    return pl.pallas_call(
        kern,
        grid=(n_tiles * nch + 2,),
        in_specs=[
            pl.BlockSpec((tm, d), lambda s: (up_tile(s), 0)),
            pl.BlockSpec((1, d), lambda s: (0, 0)),
            pl.BlockSpec((None, d, tc), lambda s: (up_chunk(s), 0, 0)),
            pl.BlockSpec((None, d, tc), lambda s: (nch + up_chunk(s), 0, 0)),
            pl.BlockSpec((None, d, tc), lambda s: (2 * nch + up_chunk(s), 0, 0)),
            pl.BlockSpec((SC_WIDTH * ROW_TILE, tc), lambda s: (0, ew_chunk(s))),
            pl.BlockSpec((tc, d), lambda s: (down_chunk(s), 0)),
        ],
        out_specs=pl.BlockSpec((tm, d), lambda s: (down_tile(s), 0)),
        out_shape=jax.ShapeDtypeStruct((t, d), F32),
        scratch_shapes=[
            pltpu.VMEM((tm, d), BF16),
            up_buf, up_buf, up_buf, pltpu.VMEM((tm, tc), BF16),
            up_buf, up_buf, up_buf, pltpu.VMEM((tm, tc), BF16),
            pltpu.VMEM((nch, ROW_TILE, tc), F32),
        ],
        compiler_params=pltpu.CompilerParams(
            dimension_semantics=("arbitrary",),
            vmem_limit_bytes=VMEM_LIMIT_BYTES),
        name="short_conv_mixer",
    )(x, nw, w_in_blocks, w_in_blocks, w_in_blocks, _tap_rows(conv_w), w_out)


def _tile(n, pref):
    c = min(pref, n)
    while n % c:
        c -= LANES
    return c


def _trunk(x, mix_norm_w, ffn_norm_w, final_norm_w,
           ssd_w_in, ssd_conv_w, ssd_conv_b, ssd_dt_bias, ssd_a_log, ssd_d, ssd_norm_w, ssd_w_out,
           sc_w_in, sc_conv_w, sc_w_out,
           ffn_w_up, ffn_conv_w, ffn_conv_b, ffn_w_down,
           *, head_dim=SSM_HEAD_DIM, n_groups=SSM_GROUPS, n_state=SSM_STATE,
           tm_proj=1024, tn_proj=1024, tm_mlp=512, tc_mlp=512):
    batch, seq, d_model = x.shape
    depth = mix_norm_w.shape[0]
    t = batch * seq
    d_inner = ssd_w_out.shape[1]
    n_heads = ssd_dt_bias.shape[1]
    main_cols = ssd_w_in.shape[2] - n_heads
    assert d_inner == n_heads * head_dim

    tm_proj = _tile(seq, tm_proj)
    tm_mlp = _tile(seq, tm_mlp)
    xs = x.reshape(t, d_model)
    for i in range(depth):
        j = i // 2
        nw = mix_norm_w[i][None, :]
        if i % 2 == 0:
            w_main = ssd_w_in[j, :, :main_cols].astype(BF16)
            w_dt = jnp.pad(ssd_w_in[j, :, main_cols:], ((0, 0), (0, LANES - n_heads))).astype(BF16)
            zxbc, dt_raw = _norm_inproj(xs, nw, w_main, w_dt, tm=tm_proj, tn=_tile(main_cols, tn_proj))
            y = _ssd_core(zxbc, dt_raw, ssd_conv_w[j], ssd_conv_b[j], ssd_dt_bias[j], ssd_a_log[j],
                          ssd_d[j], ssd_norm_w[j], batch=batch, seq=seq, d_inner=d_inner,
                          n_groups=n_groups, n_state=n_state, head_dim=head_dim)
            xs = _proj_residual(xs, y, ssd_w_out[j].astype(BF16), tm=tm_mlp, tn=_tile(d_model, tn_proj))
        else:
            xs = _short_conv(xs, nw, sc_w_in[j].astype(BF16), sc_conv_w[j], sc_w_out[j].astype(BF16),
                             seq=seq, tm=tm_mlp, tc=_tile(sc_w_out.shape[1], tc_mlp))
        xs = _conv_ffn(xs, ffn_norm_w[i][None, :], ffn_w_up[i].astype(BF16), ffn_conv_w[i],
                       ffn_conv_b[i][None, :], ffn_w_down[i].astype(BF16), final_norm_w[None, :],
                       seq=seq, tm=tm_mlp, tc=_tile(ffn_w_down.shape[1], tc_mlp),
                       final_norm=(i == depth - 1))
    return xs.reshape(batch, seq, d_model)


def kernel(x, mix_norm_w, ffn_norm_w, final_norm_w, ssd_w_in, ssd_conv_w, ssd_conv_b, ssd_dt_bias,
           ssd_a_log, ssd_d, ssd_norm_w, ssd_w_out, sc_w_in, sc_conv_w, sc_w_out,
           ffn_w_up, ffn_conv_w, ffn_conv_b, ffn_w_down):
    return _trunk(x, mix_norm_w, ffn_norm_w, final_norm_w,
                  ssd_w_in, ssd_conv_w, ssd_conv_b, ssd_dt_bias, ssd_a_log, ssd_d, ssd_norm_w, ssd_w_out,
                  sc_w_in, sc_conv_w, sc_w_out,
                  ffn_w_up, ffn_conv_w, ffn_conv_b, ffn_w_down)
```

```python
import functools

import jax
import jax.numpy as jnp
from jax import lax
from jax.experimental import pallas as pl
from jax.experimental.pallas import tpu as pltpu

F32 = jnp.float32
BF16 = jnp.bfloat16

EPS = 1e-5
SSM_HEAD_DIM = 64
SSM_GROUPS = 8
SSM_STATE = 128
SSM_CHUNK = 128
SSM_CONV = 4
SC_WIDTH = 3
FFN_CONV = 3

LANES = 128
SUBLANES = 8
VMEM_LIMIT_BYTES = 56 * 1024 * 1024

HALO = SUBLANES


def _sigmoid(v):
    return 1.0 / (1.0 + jnp.exp(-v))


def _rmsnorm_rows(x_ref, w_ref, out_ref, rows_per_step=64):
    n_rows = x_ref.shape[0]
    w = w_ref[...]

    def body(r, carry):
        sl = pl.ds(pl.multiple_of(r * rows_per_step, rows_per_step), rows_per_step)
        x = x_ref[sl, :]
        ms = jnp.mean(x * x, axis=-1, keepdims=True)
        out_ref[sl, :] = (x * lax.rsqrt(ms + EPS) * w).astype(out_ref.dtype)
        return carry

    lax.fori_loop(0, n_rows // rows_per_step, body, 0)


def _causal_conv(buf_ref, w, n_rows, width):
    acc = None
    for k in range(width):
        start = HALO - (width - 1) + k
        term = w[k:k + 1, :] * buf_ref[start:start + n_rows, :]
        acc = term if acc is None else acc + term
    return acc


def _col_blocks(w, width):
    k, n = w.shape
    return w.reshape(k, n // width, width).transpose(1, 0, 2)


def _norm_inproj_kernel(x_ref, nw_ref, w_ref, wdt_ref, o_ref, dt_ref, h_ref):
    @pl.when(pl.program_id(1) == 0)
    def _():
        _rmsnorm_rows(x_ref, nw_ref, h_ref)
        dt_ref[...] = jnp.dot(h_ref[...], wdt_ref[...], preferred_element_type=F32)

    o_ref[...] = jnp.dot(h_ref[...], w_ref[...], preferred_element_type=F32).astype(o_ref.dtype)


def _norm_inproj(x, nw, w, wdt, *, tm, tn):
    t, d = x.shape
    n = w.shape[1]
    return pl.pallas_call(
        _norm_inproj_kernel,
        grid=(t // tm, n // tn),
        in_specs=[
            pl.BlockSpec((tm, d), lambda i, j: (i, 0)),
            pl.BlockSpec((1, d), lambda i, j: (0, 0)),
            pl.BlockSpec((None, d, tn), lambda i, j: (j, 0, 0)),
            pl.BlockSpec((d, LANES), lambda i, j: (0, 0)),
        ],
        out_specs=[
            pl.BlockSpec((tm, tn), lambda i, j: (i, j)),
            pl.BlockSpec((tm, LANES), lambda i, j: (i, 0)),
        ],
        out_shape=[
            jax.ShapeDtypeStruct((t, n), BF16),
            jax.ShapeDtypeStruct((t, LANES), F32),
        ],
        scratch_shapes=[pltpu.VMEM((tm, d), BF16)],
        compiler_params=pltpu.CompilerParams(
            dimension_semantics=("arbitrary", "arbitrary"),
            vmem_limit_bytes=VMEM_LIMIT_BYTES),
        name="ssd_norm_inproj",
    )(x, nw, _col_blocks(w, tn), wdt)


def _split_bf16(v, parts):
    out = []
    rem = v
    for _ in range(parts):
        p = rem.astype(BF16)
        out.append(p)
        rem = rem - p.astype(F32)
    return out


def _ssd_core_kernel(z_ref, x_ref, bc_ref, dt_ref, cwx_ref, cbx_ref, cwbc_ref, cbbc_ref,
                     dtb_ref, alog_ref, dexp_ref, nw_ref, expand_ref, o_ref,
                     state_ref, xbuf_ref, bcbuf_ref, xc_ref, bcc_ref, xdt_ref, xw_ref, ex_ref,
                     *, n_groups, heads_per_group, head_dim, n_state):
    L = SSM_CHUNK
    gw = heads_per_group * head_dim
    bdim = n_groups * n_state

    @pl.when(pl.program_id(1) == 0)
    def _():
        state_ref[...] = jnp.zeros_like(state_ref)
        xbuf_ref[0:HALO, :] = jnp.zeros((HALO, xbuf_ref.shape[1]), F32)
        bcbuf_ref[0:HALO, :] = jnp.zeros((HALO, bcbuf_ref.shape[1]), F32)

    xbuf_ref[HALO:HALO + L, :] = x_ref[...].astype(F32)
    bcbuf_ref[HALO:HALO + L, :] = bc_ref[...].astype(F32)
    xa = _causal_conv(xbuf_ref, cwx_ref[...], L, SSM_CONV) + cbx_ref[...]
    xc_ref[...] = xa * _sigmoid(xa)
    ba = _causal_conv(bcbuf_ref, cwbc_ref[...], L, SSM_CONV) + cbbc_ref[...]
    bcc_ref[...] = (ba * _sigmoid(ba)).astype(BF16)
    xbuf_ref[0:HALO, :] = xbuf_ref[L:L + HALO, :]
    bcbuf_ref[0:HALO, :] = bcbuf_ref[L:L + HALO, :]

    dtr = dt_ref[...] + dtb_ref[...]
    dtv = jnp.maximum(dtr, 0.0) + jnp.log(1.0 + jnp.exp(-jnp.abs(dtr)))
    a = -jnp.exp(alog_ref[...])
    da = dtv * a

    row = lax.broadcasted_iota(jnp.int32, (L, L), 0)
    col = lax.broadcasted_iota(jnp.int32, (L, L), 1)
    causal = row >= col
    tril = jnp.where(causal, 1.0, 0.0).astype(BF16)
    acs = jnp.dot(jnp.concatenate([tril, tril, tril], axis=1),
                  jnp.concatenate(_split_bf16(da, 3), axis=0),
                  preferred_element_type=F32)
    acs_t = acs.T
    last = acs[L - 1:L, :]
    dte = jnp.exp(last - acs)
    cdec = jnp.broadcast_to(jnp.exp(last), (2 * SUBLANES, LANES))

    q = jnp.concatenate([dtv, dtv * dte, cdec], axis=0)
    ex_ref[...] = jnp.dot(jnp.concatenate(_split_bf16(q, 2), axis=1), expand_ref[...],
                          preferred_element_type=F32)
    xc = xc_ref[...]
    xdt_ref[...] = (xc * ex_ref[0:L, :]).astype(BF16)
    xw_ref[...] = (xc * ex_ref[L:2 * L, :]).astype(BF16)

    lane = lax.broadcasted_iota(jnp.int32, (L, LANES), 1)
    lo_half = lane < head_dim
    heads_per_tile = LANES // head_dim
    tiles_per_group = gw // LANES
    neg_inf = jnp.float32(-jnp.inf)

    for g in range(n_groups):
        gs = slice(g * gw, (g + 1) * gw)
        bg = bcc_ref[:, g * n_state:(g + 1) * n_state]
        cg = bcc_ref[:, bdim + g * n_state:bdim + (g + 1) * n_state]
        cb = lax.dot_general(cg, bg, (((1,), (1,)), ((), ())), preferred_element_type=F32)
        cg32 = cg.astype(F32)
        prev = state_ref[:, gs]
        prev_bf = prev.astype(BF16)
        y_tiles = []
        for tl in range(tiles_per_group):
            lhs_parts = []
            rhs_parts = []
            xp = xdt_ref[:, g * gw + tl * LANES:g * gw + (tl + 1) * LANES]
            pp = prev_bf[:, tl * LANES:(tl + 1) * LANES]
            for hh in range(heads_per_tile):
                h = g * heads_per_group + tl * heads_per_tile + hh
                acol = acs[:, h:h + 1]
                arow = acs_t[h:h + 1, :]
                decay = jnp.exp(jnp.where(causal, acol - arow, neg_inf))
                lhs_parts.append((cb * decay).astype(BF16))
                lhs_parts.append((cg32 * jnp.exp(acol)).astype(BF16))
                keep = lo_half if hh == 0 else jnp.logical_not(lo_half)
                rhs_parts.append(jnp.where(keep, xp, jnp.zeros_like(xp)))
                rhs_parts.append(jnp.where(keep, pp, jnp.zeros_like(pp)))
            y_tiles.append(jnp.dot(jnp.concatenate(lhs_parts, axis=1),
                                   jnp.concatenate(rhs_parts, axis=0),
                                   preferred_element_type=F32))
        y = jnp.concatenate(y_tiles, axis=1)

        s_new = lax.dot_general(bg, xw_ref[:, gs], (((0,), (0,)), ((), ())),
                                preferred_element_type=F32)
        state_ref[:, gs] = prev * ex_ref[2 * L:2 * L + 1, gs] + s_new

        yv = y + xc_ref[:, gs] * dexp_ref[:, gs]
        zg = z_ref[:, gs].astype(F32)
        yv = yv * (zg * _sigmoid(zg))
        ms = jnp.mean(yv * yv, axis=-1, keepdims=True)
        o_ref[:, gs] = (yv * lax.rsqrt(ms + EPS) * nw_ref[:, gs]).astype(o_ref.dtype)


def _ssd_core(zxbc, dt_raw, conv_w, conv_b, dt_bias, a_log, d_skip, norm_w, *,
              batch, seq, d_inner, n_groups, n_state, head_dim):
    t = zxbc.shape[0]
    n_heads = d_inner // head_dim
    heads_per_group = n_heads // n_groups
    bdim = n_groups * n_state
    L = SSM_CHUNK
    chunks = seq // L
    assert d_inner % (2 * bdim) == 0 and LANES % head_dim == 0 and n_heads <= LANES

    cwx, cwbc = conv_w[:, :d_inner], conv_w[:, d_inner:]
    cbx, cbbc = conv_b[None, :d_inner], conv_b[None, d_inner:]
    pad = LANES - n_heads
    dtb = jnp.pad(dt_bias, (0, pad))[None, :]
    alog = jnp.pad(a_log, (0, pad))[None, :]
    dexp = jnp.repeat(d_skip, head_dim)[None, :]
    nw = norm_w[None, :]
    e = (jnp.arange(LANES)[:, None] == (jnp.arange(d_inner) // head_dim)[None, :]).astype(BF16)
    expand = jnp.concatenate([e, e], axis=0)

    row_map = lambda b, c: (b * chunks + c, 0)
    const = lambda b, c: (0, 0)
    kern = functools.partial(_ssd_core_kernel, n_groups=n_groups, heads_per_group=heads_per_group,
                             head_dim=head_dim, n_state=n_state)
    return pl.pallas_call(
        kern,
        grid=(batch, chunks),
        in_specs=[
            pl.BlockSpec((L, d_inner), row_map),
            pl.BlockSpec((L, d_inner), lambda b, c: (b * chunks + c, 1)),
            pl.BlockSpec((L, 2 * bdim), lambda b, c: (b * chunks + c, d_inner // bdim)),
            pl.BlockSpec((L, LANES), row_map),
            pl.BlockSpec((SSM_CONV, d_inner), const),
            pl.BlockSpec((1, d_inner), const),
            pl.BlockSpec((SSM_CONV, 2 * bdim), const),
            pl.BlockSpec((1, 2 * bdim), const),
            pl.BlockSpec((1, LANES), const),
            pl.BlockSpec((1, LANES), const),
            pl.BlockSpec((1, d_inner), const),
            pl.BlockSpec((1, d_inner), const),
            pl.BlockSpec((2 * LANES, d_inner), const),
        ],
        out_specs=pl.BlockSpec((L, d_inner), row_map),
        out_shape=jax.ShapeDtypeStruct((t, d_inner), BF16),
        scratch_shapes=[
            pltpu.VMEM((n_state, d_inner), F32),
            pltpu.VMEM((HALO + L, d_inner), F32),
            pltpu.VMEM((HALO + L, 2 * bdim), F32),
            pltpu.VMEM((L, d_inner), F32),
            pltpu.VMEM((L, 2 * bdim), BF16),
            pltpu.VMEM((L, d_inner), BF16),
            pltpu.VMEM((L, d_inner), BF16),
            pltpu.VMEM((2 * L + 2 * SUBLANES, d_inner), F32),
        ],
        compiler_params=pltpu.CompilerParams(
            dimension_semantics=("arbitrary", "arbitrary"),
            vmem_limit_bytes=VMEM_LIMIT_BYTES),
        name="ssd_core",
    )(zxbc, zxbc, zxbc, dt_raw, cwx, cbx, cwbc, cbbc, dtb, alog, dexp, nw, expand)


def _proj_residual_kernel(x_ref, y_ref, w_ref, o_ref):
    o_ref[...] = x_ref[...] + jnp.dot(y_ref[...], w_ref[...], preferred_element_type=F32)


def _proj_residual(x, y, w, *, tm, tn):
    t, d = x.shape
    k = y.shape[1]
    return pl.pallas_call(
        _proj_residual_kernel,
        grid=(d // tn, t // tm),
        in_specs=[
            pl.BlockSpec((tm, tn), lambda j, i: (i, j)),
            pl.BlockSpec((tm, k), lambda j, i: (i, 0)),
            pl.BlockSpec((None, k, tn), lambda j, i: (j, 0, 0)),
        ],
        out_specs=pl.BlockSpec((tm, tn), lambda j, i: (i, j)),
        out_shape=jax.ShapeDtypeStruct((t, d), F32),
        compiler_params=pltpu.CompilerParams(
            dimension_semantics=("arbitrary", "arbitrary"),
            vmem_limit_bytes=VMEM_LIMIT_BYTES),
        name="ssd_outproj_residual",
    )(x, y, _col_blocks(w, tn))


ROW_TILE = SUBLANES
COL_BLOCK = 2 * LANES


def _mlp_prologue(x_ref, nw_ref, o_ref, h_ref, zero_refs, *, nch, n_pairs):
    s = pl.program_id(0)

    @pl.when(s == 0)
    def _():
        for r in zero_refs:
            r[...] = jnp.zeros(r.shape, r.dtype)
        o_ref[...] = jnp.zeros(o_ref.shape, o_ref.dtype)

    @pl.when(jnp.logical_and(s % nch == 0, s < n_pairs))
    def _():
        _rmsnorm_rows(x_ref, nw_ref, h_ref)

    @pl.when(jnp.logical_and((s - 2) % nch == 0, s >= 2))
    def _():
        o_ref[...] = x_ref[...]


def _mlp_epilogue(o_ref, fnw_ref, *, nch, final_norm):
    if final_norm:
        s = pl.program_id(0)

        @pl.when(jnp.logical_and((s - 2) % nch == nch - 1, s >= 2))
        def _():
            _rmsnorm_rows(o_ref, fnw_ref, o_ref)


def _by_parity(body, set_a, set_b):
    s = pl.program_id(0)

    @pl.when(s % 2 == 0)
    def _():
        body(set_a, set_b)

    @pl.when(s % 2 == 1)
    def _():
        body(set_b, set_a)


def _elementwise_stage_ids(nch, tiles_per_seq):
    pair = jnp.maximum(pl.program_id(0) - 1, 0)
    return pair % nch, (pair // nch) % tiles_per_seq == 0


class _ConvTaps:
    def __init__(self, par_ref, chunk, row0, cols, history):
        self.w = [par_ref[chunk, row0 + ROW_TILE * k:row0 + ROW_TILE * (k + 1), cols] for k in range(3)]
        row = lax.broadcasted_iota(jnp.int32, history.shape, 0)
        self.keep = [row >= 1, row >= 2]
        self.rolled = [pltpu.roll(history, 1, 0), pltpu.roll(history, 2, 0)]

    def __call__(self, u):
        rolled = [pltpu.roll(u, 1, 0), pltpu.roll(u, 2, 0)]
        sh1 = jnp.where(self.keep[0], rolled[0], self.rolled[0])
        sh2 = jnp.where(self.keep[1], rolled[1], self.rolled[1])
        self.rolled = rolled
        return self.w[2] * u + self.w[1] * sh1 + self.w[0] * sh2


def _history(halo_ref, chunk, cols, seq_start):
    hist = halo_ref[chunk, :, cols]
    return jnp.where(seq_start, jnp.zeros_like(hist), hist)


def _up_project(up_ref, h_ref, w_ref):
    n_parts, _, tc = w_ref.shape
    h = h_ref[...]
    for p in range(n_parts - 1):
        up_ref[:, p * tc:(p + 1) * tc] = jnp.dot(h, w_ref[p], preferred_element_type=F32)


def _down_project(o_ref, act_ref, w_ref):
    n_parts, _, tc = w_ref.shape
    act = act_ref[...]
    for q in range(o_ref.shape[1] // tc):
        blk = slice(q * tc, (q + 1) * tc)
        o_ref[:, blk] += jnp.dot(act, w_ref[n_parts - 1, blk, :], preferred_element_type=F32)


FFN_TAP_ROWS = FFN_CONV * ROW_TILE
FFN_PAR_ROWS = 2 * FFN_TAP_ROWS + 2 * ROW_TILE


def _ffn_elementwise(up_ref, act_ref, par_ref, ghalo_ref, vhalo_ref, chunk, seq_start):
    tm, tc = act_ref.shape
    cb = min(COL_BLOCK, tc)
    for c0 in range(0, tc, cb):
        gcols = slice(c0, c0 + cb)
        vcols = slice(tc + c0, tc + c0 + cb)
        conv_g = _ConvTaps(par_ref, chunk, 0, gcols, _history(ghalo_ref, chunk, gcols, seq_start))
        conv_v = _ConvTaps(par_ref, chunk, FFN_TAP_ROWS, gcols, _history(vhalo_ref, chunk, gcols, seq_start))
        bias_g = par_ref[chunk, 2 * FFN_TAP_ROWS:2 * FFN_TAP_ROWS + ROW_TILE, gcols]
        bias_v = par_ref[chunk, 2 * FFN_TAP_ROWS + ROW_TILE:FFN_PAR_ROWS, gcols]
        for r0 in range(0, tm, 2 * ROW_TILE):
            halves = []
            for r in (r0, r0 + ROW_TILE):
                g = conv_g(up_ref[r:r + ROW_TILE, gcols]) + bias_g
                v = conv_v(up_ref[r:r + ROW_TILE, vcols]) + bias_v
                halves.append(g * _sigmoid(g) * v)
            act_ref[r0:r0 + 2 * ROW_TILE, gcols] = jnp.concatenate(halves, axis=0).astype(BF16)
        ghalo_ref[chunk, :, gcols] = up_ref[tm - ROW_TILE:tm, gcols]
        vhalo_ref[chunk, :, gcols] = up_ref[tm - ROW_TILE:tm, vcols]


def _ffn_kernel(x_ref, nw_ref, w_ref, par_ref, fnw_ref,
                o_ref, h_ref, up0_ref, a0_ref, up1_ref, a1_ref, ghalo_ref, vhalo_ref,
                *, nch, n_pairs, tiles_per_seq, final_norm):
    chunk, seq_start = _elementwise_stage_ids(nch, tiles_per_seq)
    _mlp_prologue(x_ref, nw_ref, o_ref, h_ref, (up0_ref, a0_ref, up1_ref, a1_ref, ghalo_ref, vhalo_ref),
                  nch=nch, n_pairs=n_pairs)

    def body(cur, prv):
        up, act_down = cur
        ew_up, act_ew = prv
        _ffn_elementwise(ew_up, act_ew, par_ref, ghalo_ref, vhalo_ref, chunk, seq_start)
        _up_project(up, h_ref, w_ref)
        _down_project(o_ref, act_down, w_ref)

    _by_parity(body, (up0_ref, a0_ref), (up1_ref, a1_ref))
    _mlp_epilogue(o_ref, fnw_ref, nch=nch, final_norm=final_norm)


def _tap_rows(conv_w):
    return jnp.repeat(conv_w, ROW_TILE, axis=0)


def _down_blocks(w_down, tc):
    n, d = w_down.shape
    return w_down.reshape(n // tc, tc, d // tc, tc).transpose(0, 2, 1, 3).reshape(n // tc, d, tc)


def _step_slabs(up_parts, w_down, tc):
    down = jnp.roll(_down_blocks(w_down, tc), 2, axis=0)
    return jnp.stack(list(up_parts) + [down], axis=1)


def _mlp_call(kern, name, x, nw, slabs, params, extra_in, extra_specs, scratch, *, tm, nch):
    t, d = x.shape
    n_tiles = t // tm
    up_tile = lambda s: jnp.minimum(s // nch, n_tiles - 1)
    down_tile = lambda s: jnp.maximum(s - 2, 0) // nch
    return pl.pallas_call(
        kern,
        grid=(n_tiles * nch + 2,),
        in_specs=[
            pl.BlockSpec((tm, d), lambda s: (up_tile(s), 0)),
            pl.BlockSpec((1, d), lambda s: (0, 0)),
            pl.BlockSpec((None,) + slabs.shape[1:], lambda s: (s % nch, 0, 0, 0)),
            pl.BlockSpec(params.shape, lambda s: (0, 0, 0)),
        ] + extra_specs,
        out_specs=pl.BlockSpec((tm, d), lambda s: (down_tile(s), 0)),
        out_shape=jax.ShapeDtypeStruct((t, d), F32),
        scratch_shapes=[pltpu.VMEM((tm, d), BF16)] + scratch,
        compiler_params=pltpu.CompilerParams(
            dimension_semantics=("arbitrary",),
            vmem_limit_bytes=VMEM_LIMIT_BYTES),
        name=name,
    )(x, nw, slabs, params, *extra_in)


def _conv_ffn(x, nw, w_up, conv_w, conv_b, w_down, fnw, *, seq, tm, tc, final_norm):
    t, d = x.shape
    d_ff = w_down.shape[0]
    nch = d_ff // tc
    assert nch >= 3 and tc % min(COL_BLOCK, tc) == 0 and tm % (2 * ROW_TILE) == 0 and d % tc == 0
    kern = functools.partial(_ffn_kernel, nch=nch, n_pairs=(t // tm) * nch, tiles_per_seq=seq // tm,
                             final_norm=final_norm)
    up = _col_blocks(w_up, tc)
    slabs = _step_slabs((up[:nch], up[nch:]), w_down, tc)
    taps = _col_blocks(_tap_rows(conv_w), tc)
    bias = _col_blocks(jnp.broadcast_to(conv_b, (ROW_TILE, conv_b.shape[1])), tc)
    params = jnp.concatenate([taps[:nch], taps[nch:], bias[:nch], bias[nch:]], axis=1)
    up_buf = pltpu.VMEM((tm, 2 * tc), F32)
    act_buf = pltpu.VMEM((tm, tc), BF16)
    halo = pltpu.VMEM((nch, ROW_TILE, tc), F32)
    return _mlp_call(kern, "conv_ffn", x, nw, slabs, params, [fnw],
                     [pl.BlockSpec((1, d), lambda s: (0, 0))],
                     [up_buf, act_buf, up_buf, act_buf, halo, halo], tm=tm, nch=nch)


def _sc_elementwise(up_ref, act_ref, par_ref, uhalo_ref, chunk, seq_start):
    tm, tc = act_ref.shape
    cb = min(COL_BLOCK, tc)
    for c0 in range(0, tc, cb):
        bcols = slice(c0, c0 + cb)
        ccols = slice(tc + c0, tc + c0 + cb)
        hcols = slice(2 * tc + c0, 2 * tc + c0 + cb)
        conv = _ConvTaps(par_ref, chunk, 0, bcols, _history(uhalo_ref, chunk, bcols, seq_start))
        u = None
        for r0 in range(0, tm, 2 * ROW_TILE):
            halves = []
            for r in (r0, r0 + ROW_TILE):
                u = up_ref[r:r + ROW_TILE, ccols] * up_ref[r:r + ROW_TILE, hcols]
                halves.append(up_ref[r:r + ROW_TILE, bcols] * conv(u))
            act_ref[r0:r0 + 2 * ROW_TILE, bcols] = jnp.concatenate(halves, axis=0).astype(BF16)
        uhalo_ref[chunk, :, bcols] = u


def _sc_kernel(x_ref, nw_ref, w_ref, par_ref,
               o_ref, h_ref, up0_ref, a0_ref, up1_ref, a1_ref, uhalo_ref, *, nch, n_pairs, tiles_per_seq):
    chunk, seq_start = _elementwise_stage_ids(nch, tiles_per_seq)
    _mlp_prologue(x_ref, nw_ref, o_ref, h_ref, (up0_ref, a0_ref, up1_ref, a1_ref, uhalo_ref),
                  nch=nch, n_pairs=n_pairs)

    def body(cur, prv):
        up, act_down = cur
        ew_up, act_ew = prv
        _sc_elementwise(ew_up, act_ew, par_ref, uhalo_ref, chunk, seq_start)
        _up_project(up, h_ref, w_ref)
        _down_project(o_ref, act_down, w_ref)

    _by_parity(body, (up0_ref, a0_ref), (up1_ref, a1_ref))


def _short_conv(x, nw, w_in, conv_w, w_out, *, seq, tm, tc):
    t, d = x.shape
    sc_dim = w_out.shape[0]
    nch = sc_dim // tc
    assert nch >= 3 and tc % min(COL_BLOCK, tc) == 0 and tm % (2 * ROW_TILE) == 0 and d % tc == 0
    kern = functools.partial(_sc_kernel, nch=nch, n_pairs=(t // tm) * nch, tiles_per_seq=seq // tm)
    up = _col_blocks(w_in, tc)
    slabs = _step_slabs((up[:nch], up[nch:2 * nch], up[2 * nch:]), w_out, tc)
    params = _col_blocks(_tap_rows(conv_w), tc)
    up_buf = pltpu.VMEM((tm, 3 * tc), F32)
    act_buf = pltpu.VMEM((tm, tc), BF16)
    return _mlp_call(kern, "short_conv_mixer", x, nw, slabs, params, [], [],
                     [up_buf, act_buf, up_buf, act_buf, pltpu.VMEM((nch, ROW_TILE, tc), F32)],
                     tm=tm, nch=nch)


def _tile(n, pref):
    c = min(pref, n)
    while n % c:
        c -= LANES
    return c


def _trunk(x, mix_norm_w, ffn_norm_w, final_norm_w,
           ssd_w_in, ssd_conv_w, ssd_conv_b, ssd_dt_bias, ssd_a_log, ssd_d, ssd_norm_w, ssd_w_out,
           sc_w_in, sc_conv_w, sc_w_out,
           ffn_w_up, ffn_conv_w, ffn_conv_b, ffn_w_down,
           *, head_dim=SSM_HEAD_DIM, n_groups=SSM_GROUPS, n_state=SSM_STATE,
           tm_proj=1024, tn_proj=1024, tm_mlp=512, tc_mlp=512):
    batch, seq, d_model = x.shape
    depth = mix_norm_w.shape[0]
    t = batch * seq
    d_inner = ssd_w_out.shape[1]
    n_heads = ssd_dt_bias.shape[1]
    main_cols = ssd_w_in.shape[2] - n_heads
    assert d_inner == n_heads * head_dim

    tm_proj = _tile(seq, tm_proj)
    tm_mlp = _tile(seq, tm_mlp)
    xs = x.reshape(t, d_model)
    for i in range(depth):
        j = i // 2
        nw = mix_norm_w[i][None, :]
        if i % 2 == 0:
            w_main = ssd_w_in[j, :, :main_cols].astype(BF16)
            w_dt = jnp.pad(ssd_w_in[j, :, main_cols:], ((0, 0), (0, LANES - n_heads))).astype(BF16)
            zxbc, dt_raw = _norm_inproj(xs, nw, w_main, w_dt, tm=tm_proj, tn=_tile(main_cols, tn_proj))
            y = _ssd_core(zxbc, dt_raw, ssd_conv_w[j], ssd_conv_b[j], ssd_dt_bias[j], ssd_a_log[j],
                          ssd_d[j], ssd_norm_w[j], batch=batch, seq=seq, d_inner=d_inner,
                          n_groups=n_groups, n_state=n_state, head_dim=head_dim)
            xs = _proj_residual(xs, y, ssd_w_out[j].astype(BF16), tm=tm_mlp, tn=_tile(d_model, tn_proj))
        else:
            xs = _short_conv(xs, nw, sc_w_in[j].astype(BF16), sc_conv_w[j], sc_w_out[j].astype(BF16),
                             seq=seq, tm=tm_mlp, tc=_tile(sc_w_out.shape[1], tc_mlp))
        xs = _conv_ffn(xs, ffn_norm_w[i][None, :], ffn_w_up[i].astype(BF16), ffn_conv_w[i],
                       ffn_conv_b[i][None, :], ffn_w_down[i].astype(BF16), final_norm_w[None, :],
                       seq=seq, tm=tm_mlp, tc=_tile(ffn_w_down.shape[1], tc_mlp),
                       final_norm=(i == depth - 1))
    return xs.reshape(batch, seq, d_model)


def kernel(x, mix_norm_w, ffn_norm_w, final_norm_w, ssd_w_in, ssd_conv_w, ssd_conv_b, ssd_dt_bias,
           ssd_a_log, ssd_d, ssd_norm_w, ssd_w_out, sc_w_in, sc_conv_w, sc_w_out,
           ffn_w_up, ffn_conv_w, ffn_conv_b, ffn_w_down):
    return _trunk(x, mix_norm_w, ffn_norm_w, final_norm_w,
                  ssd_w_in, ssd_conv_w, ssd_conv_b, ssd_dt_bias, ssd_a_log, ssd_d, ssd_norm_w, ssd_w_out,
                  sc_w_in, sc_conv_w, sc_w_out,
                  ffn_w_up, ffn_conv_w, ffn_conv_b, ffn_w_down)
```

```python
import functools

import jax
import jax.numpy as jnp
from jax import lax
from jax.experimental import pallas as pl
from jax.experimental.pallas import tpu as pltpu

F32 = jnp.float32
BF16 = jnp.bfloat16

EPS = 1e-5
SSM_HEAD_DIM = 64
SSM_GROUPS = 8
SSM_STATE = 128
SSM_CHUNK = 128
SSM_CONV = 4
SC_WIDTH = 3
FFN_CONV = 3

LANES = 128
SUBLANES = 8
VMEM_LIMIT_BYTES = 56 * 1024 * 1024

NEG_LOG2_E = -1.4426950408889634


def _sigmoid(v):
    return 1.0 / (1.0 + jnp.exp2(v * NEG_LOG2_E))


def _rmsnorm_rows(x_ref, w_ref, out_ref, rows_per_step=64):
    n_rows = x_ref.shape[0]
    w = w_ref[...]

    def body(r, carry):
        sl = pl.ds(pl.multiple_of(r * rows_per_step, rows_per_step), rows_per_step)
        x = x_ref[sl, :]
        ms = jnp.mean(x * x, axis=-1, keepdims=True)
        out_ref[sl, :] = (x * lax.rsqrt(ms + EPS) * w).astype(out_ref.dtype)
        return carry

    lax.fori_loop(0, n_rows // rows_per_step, body, 0)


ROW_TILE = SUBLANES
COL_BLOCK = 2 * LANES


class _ConvTaps:
    def __init__(self, taps, history):
        self.taps = taps
        shifts = range(1, len(taps))
        row = lax.broadcasted_iota(jnp.int32, history.shape, 0)
        self.keep = [row >= j for j in shifts]
        self.rolled = [pltpu.roll(history, j, 0) for j in shifts]

    def __call__(self, u):
        n = len(self.taps)
        rolled = [pltpu.roll(u, j, 0) for j in range(1, n)]
        acc = self.taps[n - 1] * u
        for j in range(1, n):
            acc = acc + self.taps[n - 1 - j] * jnp.where(self.keep[j - 1], rolled[j - 1], self.rolled[j - 1])
        self.rolled = rolled
        return acc


def _tap_tiles(par_ref, row0, n_taps, cols):
    return [par_ref[row0 + ROW_TILE * k:row0 + ROW_TILE * (k + 1), cols] for k in range(n_taps)]


def _tap_rows(conv_w):
    return jnp.repeat(conv_w, ROW_TILE, axis=0)


def _norm_inproj_kernel(x_ref, nw_ref, w_ref, wdt_ref, o_ref, dt_ref, h_ref):
    @pl.when(pl.program_id(1) == 0)
    def _():
        _rmsnorm_rows(x_ref, nw_ref, h_ref)
        dt_ref[...] = jnp.dot(h_ref[...], wdt_ref[...], preferred_element_type=F32)

    o_ref[...] = jnp.dot(h_ref[...], w_ref[...], preferred_element_type=F32).astype(o_ref.dtype)


def _norm_inproj(x, nw, w, wdt, *, tm, tn):
    t, d = x.shape
    n = w.shape[1]
    return pl.pallas_call(
        _norm_inproj_kernel,
        grid=(t // tm, n // tn),
        in_specs=[
            pl.BlockSpec((tm, d), lambda i, j: (i, 0)),
            pl.BlockSpec((1, d), lambda i, j: (0, 0)),
            pl.BlockSpec((d, tn), lambda i, j: (0, j)),
            pl.BlockSpec((d, LANES), lambda i, j: (0, 0)),
        ],
        out_specs=[
            pl.BlockSpec((tm, tn), lambda i, j: (i, j)),
            pl.BlockSpec((tm, LANES), lambda i, j: (i, 0)),
        ],
        out_shape=[
            jax.ShapeDtypeStruct((t, n), BF16),
            jax.ShapeDtypeStruct((t, LANES), F32),
        ],
        scratch_shapes=[pltpu.VMEM((tm, d), BF16)],
        compiler_params=pltpu.CompilerParams(
            dimension_semantics=("arbitrary", "arbitrary"),
            vmem_limit_bytes=VMEM_LIMIT_BYTES),
        name="ssd_norm_inproj",
    )(x, nw, w, wdt)


def _split_bf16(v, parts):
    out = []
    rem = v
    for _ in range(parts):
        p = rem.astype(BF16)
        out.append(p)
        rem = rem - p.astype(F32)
    return out


def _ssd_core_kernel(z_ref, x_ref, bc_ref, dt_ref, cpar_ref,
                     dtb_ref, alog_ref, dexp_ref, nw_ref, expand_ref, o_ref,
                     state_ref, hist_ref, xc_ref, bcc_ref, xdt_ref, xw_ref, ex_ref,
                     *, n_groups, heads_per_group, head_dim, n_state):
    L = SSM_CHUNK
    gw = heads_per_group * head_dim
    bdim = n_groups * n_state

    @pl.when(pl.program_id(1) == 0)
    def _():
        state_ref[...] = jnp.zeros_like(state_ref)
        hist_ref[...] = jnp.zeros_like(hist_ref)

    d_in = x_ref.shape[1]
    bias_row0 = SSM_CONV * ROW_TILE
    for src_ref, col0 in ((x_ref, 0), (bc_ref, d_in)):
        for c0 in range(0, src_ref.shape[1], COL_BLOCK):
            cols = slice(c0, c0 + COL_BLOCK)
            pcols = slice(col0 + c0, col0 + c0 + COL_BLOCK)
            conv = _ConvTaps(_tap_tiles(cpar_ref, 0, SSM_CONV, pcols), hist_ref[:, pcols])
            bias = cpar_ref[bias_row0:bias_row0 + ROW_TILE, pcols]
            tile = None
            for r0 in range(0, L, 2 * ROW_TILE):
                blk = src_ref[r0:r0 + 2 * ROW_TILE, cols].astype(F32)
                halves = []
                for tile in (blk[0:ROW_TILE], blk[ROW_TILE:2 * ROW_TILE]):
                    a = conv(tile) + bias
                    halves.append(a * _sigmoid(a))
                if src_ref is x_ref:
                    xc_ref[r0:r0 + 2 * ROW_TILE, cols] = jnp.concatenate(halves, axis=0)
                else:
                    bcc_ref[r0:r0 + 2 * ROW_TILE, cols] = jnp.concatenate(halves, axis=0).astype(BF16)
            hist_ref[:, pcols] = tile

    dtr = dt_ref[...] + dtb_ref[...]
    dtv = jnp.maximum(dtr, 0.0) + jnp.log(1.0 + jnp.exp(-jnp.abs(dtr)))
    a = -jnp.exp(alog_ref[...])
    da = dtv * a

    row = lax.broadcasted_iota(jnp.int32, (L, L), 0)
    col = lax.broadcasted_iota(jnp.int32, (L, L), 1)
    causal = row >= col
    tril = jnp.where(causal, 1.0, 0.0).astype(BF16)
    acs = jnp.dot(jnp.concatenate([tril, tril, tril], axis=1),
                  jnp.concatenate(_split_bf16(da, 3), axis=0),
                  preferred_element_type=F32)
    acs_t = acs.T
    last = acs[L - 1:L, :]
    dte = jnp.exp(last - acs)
    cdec = jnp.broadcast_to(jnp.exp(last), (2 * SUBLANES, LANES))

    q = jnp.concatenate([dtv, dtv * dte, cdec], axis=0)
    ex_ref[...] = jnp.dot(jnp.concatenate(_split_bf16(q, 2), axis=1), expand_ref[...],
                          preferred_element_type=F32)
    xc = xc_ref[...]
    xdt_ref[...] = (xc * ex_ref[0:L, :]).astype(BF16)
    xw_ref[...] = (xc * ex_ref[L:2 * L, :]).astype(BF16)

    lane = lax.broadcasted_iota(jnp.int32, (L, LANES), 1)
    lo_half = lane < head_dim
    heads_per_tile = LANES // head_dim
    tiles_per_group = gw // LANES
    neg_inf = jnp.float32(-jnp.inf)

    for g in range(n_groups):
        gs = slice(g * gw, (g + 1) * gw)
        bg = bcc_ref[:, g * n_state:(g + 1) * n_state]
        cg = bcc_ref[:, bdim + g * n_state:bdim + (g + 1) * n_state]
        cb = lax.dot_general(cg, bg, (((1,), (1,)), ((), ())), preferred_element_type=F32)
        cg32 = cg.astype(F32)
        prev = state_ref[:, gs]
        prev_bf = prev.astype(BF16)
        y_tiles = []
        for tl in range(tiles_per_group):
            lhs_parts = []
            rhs_parts = []
            xp = xdt_ref[:, g * gw + tl * LANES:g * gw + (tl + 1) * LANES]
            pp = prev_bf[:, tl * LANES:(tl + 1) * LANES]
            for hh in range(heads_per_tile):
                h = g * heads_per_group + tl * heads_per_tile + hh
                acol = acs[:, h:h + 1]
                arow = acs_t[h:h + 1, :]
                decay = jnp.exp(jnp.where(causal, acol - arow, neg_inf))
                lhs_parts.append((cb * decay).astype(BF16))
                lhs_parts.append((cg32 * jnp.exp(acol)).astype(BF16))
                keep = lo_half if hh == 0 else jnp.logical_not(lo_half)
                rhs_parts.append(jnp.where(keep, xp, jnp.zeros_like(xp)))
                rhs_parts.append(jnp.where(keep, pp, jnp.zeros_like(pp)))
            y_tiles.append(jnp.dot(jnp.concatenate(lhs_parts, axis=1),
                                   jnp.concatenate(rhs_parts, axis=0),
                                   preferred_element_type=F32))
        y = jnp.concatenate(y_tiles, axis=1)

        s_new = lax.dot_general(bg, xw_ref[:, gs], (((0,), (0,)), ((), ())),
                                preferred_element_type=F32)
        state_ref[:, gs] = prev * ex_ref[2 * L:2 * L + 1, gs] + s_new

        yv = y + xc_ref[:, gs] * dexp_ref[:, gs]
        zg = z_ref[:, gs].astype(F32)
        yv = yv * (zg * _sigmoid(zg))
        ms = jnp.mean(yv * yv, axis=-1, keepdims=True)
        o_ref[:, gs] = (yv * lax.rsqrt(ms + EPS) * nw_ref[:, gs]).astype(o_ref.dtype)


def _ssd_core(zxbc, dt_raw, conv_w, conv_b, dt_bias, a_log, d_skip, norm_w, *,
              batch, seq, d_inner, n_groups, n_state, head_dim):
    t = zxbc.shape[0]
    n_heads = d_inner // head_dim
    heads_per_group = n_heads // n_groups
    bdim = n_groups * n_state
    L = SSM_CHUNK
    chunks = seq // L
    assert d_inner % (2 * bdim) == 0 and LANES % head_dim == 0 and n_heads <= LANES
    assert d_inner % COL_BLOCK == 0 and (2 * bdim) % COL_BLOCK == 0

    conv_dim = conv_w.shape[1]
    cpar = jnp.concatenate([_tap_rows(conv_w), jnp.broadcast_to(conv_b[None, :], (ROW_TILE, conv_dim))], axis=0)
    pad = LANES - n_heads
    dtb = jnp.pad(dt_bias, (0, pad))[None, :]
    alog = jnp.pad(a_log, (0, pad))[None, :]
    dexp = jnp.repeat(d_skip, head_dim)[None, :]
    nw = norm_w[None, :]
    e = (jnp.arange(LANES)[:, None] == (jnp.arange(d_inner) // head_dim)[None, :]).astype(BF16)
    expand = jnp.concatenate([e, e], axis=0)

    row_map = lambda b, c: (b * chunks + c, 0)
    const = lambda b, c: (0, 0)
    kern = functools.partial(_ssd_core_kernel, n_groups=n_groups, heads_per_group=heads_per_group,
                             head_dim=head_dim, n_state=n_state)
    return pl.pallas_call(
        kern,
        grid=(batch, chunks),
        in_specs=[
            pl.BlockSpec((L, d_inner), row_map),
            pl.BlockSpec((L, d_inner), lambda b, c: (b * chunks + c, 1)),
            pl.BlockSpec((L, 2 * bdim), lambda b, c: (b * chunks + c, d_inner // bdim)),
            pl.BlockSpec((L, LANES), row_map),
            pl.BlockSpec(((SSM_CONV + 1) * ROW_TILE, conv_dim), const),
            pl.BlockSpec((1, LANES), const),
            pl.BlockSpec((1, LANES), const),
            pl.BlockSpec((1, d_inner), const),
            pl.BlockSpec((1, d_inner), const),
            pl.BlockSpec((2 * LANES, d_inner), const),
        ],
        out_specs=pl.BlockSpec((L, d_inner), row_map),
        out_shape=jax.ShapeDtypeStruct((t, d_inner), BF16),
        scratch_shapes=[
            pltpu.VMEM((n_state, d_inner), F32),
            pltpu.VMEM((ROW_TILE, conv_dim), F32),
            pltpu.VMEM((L, d_inner), F32),
            pltpu.VMEM((L, 2 * bdim), BF16),
            pltpu.VMEM((L, d_inner), BF16),
            pltpu.VMEM((L, d_inner), BF16),
            pltpu.VMEM((2 * L + 2 * SUBLANES, d_inner), F32),
        ],
        compiler_params=pltpu.CompilerParams(
            dimension_semantics=("arbitrary", "arbitrary"),
            vmem_limit_bytes=VMEM_LIMIT_BYTES),
        name="ssd_core",
    )(zxbc, zxbc, zxbc, dt_raw, cpar, dtb, alog, dexp, nw, expand)


def _proj_residual_kernel(x_ref, y_ref, w_ref, o_ref):
    o_ref[...] = x_ref[...] + jnp.dot(y_ref[...], w_ref[...], preferred_element_type=F32)


def _proj_residual(x, y, w, *, tm, tn):
    t, d = x.shape
    k = y.shape[1]
    return pl.pallas_call(
        _proj_residual_kernel,
        grid=(d // tn, t // tm),
        in_specs=[
            pl.BlockSpec((tm, tn), lambda j, i: (i, j)),
            pl.BlockSpec((tm, k), lambda j, i: (i, 0)),
            pl.BlockSpec((k, tn), lambda j, i: (0, j)),
        ],
        out_specs=pl.BlockSpec((tm, tn), lambda j, i: (i, j)),
        out_shape=jax.ShapeDtypeStruct((t, d), F32),
        compiler_params=pltpu.CompilerParams(
            dimension_semantics=("arbitrary", "arbitrary"),
            vmem_limit_bytes=VMEM_LIMIT_BYTES),
        name="ssd_outproj_residual",
    )(x, y, w)


def _mlp_prologue(x_ref, nw_ref, o_ref, h_ref, zero_refs, *, nch, n_pairs):
    s = pl.program_id(0)

    @pl.when(s == 0)
    def _():
        for r in zero_refs:
            r[...] = jnp.zeros(r.shape, r.dtype)
        o_ref[...] = jnp.zeros(o_ref.shape, o_ref.dtype)

    @pl.when(jnp.logical_and(s % nch == 0, s < n_pairs))
    def _():
        _rmsnorm_rows(x_ref, nw_ref, h_ref)

    @pl.when(jnp.logical_and((s - 2) % nch == 0, s >= 2))
    def _():
        o_ref[...] = x_ref[...]


def _mlp_epilogue(o_ref, fnw_ref, *, nch, final_norm):
    if final_norm:
        s = pl.program_id(0)

        @pl.when(jnp.logical_and((s - 2) % nch == nch - 1, s >= 2))
        def _():
            _rmsnorm_rows(o_ref, fnw_ref, o_ref)


def _by_parity(body, set_a, set_b):
    s = pl.program_id(0)

    @pl.when(s % 2 == 0)
    def _():
        body(set_a, set_b)

    @pl.when(s % 2 == 1)
    def _():
        body(set_b, set_a)


def _elementwise_stage_ids(nch, tiles_per_seq):
    pair = jnp.maximum(pl.program_id(0) - 1, 0)
    return pair % nch, (pair // nch) % tiles_per_seq == 0


def _history(halo_ref, chunk, cols, seq_start):
    hist = halo_ref[chunk, :, cols]
    return jnp.where(seq_start, jnp.zeros_like(hist), hist)


def _ffn_elementwise(g_ref, v_ref, act_ref, cwg_ref, cwv_ref, cbg_ref, cbv_ref, ghalo_ref, vhalo_ref,
                     chunk, seq_start):
    tm, tc = g_ref.shape
    for c0 in range(0, tc, min(COL_BLOCK, tc)):
        cols = slice(c0, c0 + min(COL_BLOCK, tc))
        conv_g = _ConvTaps(_tap_tiles(cwg_ref, 0, FFN_CONV, cols), _history(ghalo_ref, chunk, cols, seq_start))
        conv_v = _ConvTaps(_tap_tiles(cwv_ref, 0, FFN_CONV, cols), _history(vhalo_ref, chunk, cols, seq_start))
        bias_g = cbg_ref[:, cols]
        bias_v = cbv_ref[:, cols]
        for r0 in range(0, tm, 2 * ROW_TILE):
            halves = []
            for r in (r0, r0 + ROW_TILE):
                g = conv_g(g_ref[r:r + ROW_TILE, cols]) + bias_g
                v = conv_v(v_ref[r:r + ROW_TILE, cols]) + bias_v
                halves.append(g * _sigmoid(g) * v)
            act_ref[r0:r0 + 2 * ROW_TILE, cols] = jnp.concatenate(halves, axis=0).astype(BF16)
        ghalo_ref[chunk, :, cols] = g_ref[tm - ROW_TILE:tm, cols]
        vhalo_ref[chunk, :, cols] = v_ref[tm - ROW_TILE:tm, cols]


def _ffn_kernel(x_ref, nw_ref, wg_ref, wv_ref, cwg_ref, cwv_ref, cbg_ref, cbv_ref, w2_ref, fnw_ref,
                o_ref, h_ref, g0_ref, v0_ref, a0_ref, g1_ref, v1_ref, a1_ref, ghalo_ref, vhalo_ref,
                *, nch, n_pairs, tiles_per_seq, final_norm):
    chunk, seq_start = _elementwise_stage_ids(nch, tiles_per_seq)
    _mlp_prologue(x_ref, nw_ref, o_ref, h_ref,
                  (g0_ref, v0_ref, a0_ref, g1_ref, v1_ref, a1_ref, ghalo_ref, vhalo_ref),
                  nch=nch, n_pairs=n_pairs)

    def body(cur, prv):
        up_g, up_v, act_down = cur
        ew_g, ew_v, act_ew = prv
        _ffn_elementwise(ew_g, ew_v, act_ew, cwg_ref, cwv_ref, cbg_ref, cbv_ref, ghalo_ref, vhalo_ref,
                         chunk, seq_start)
        h = h_ref[...]
        up_g[...] = jnp.dot(h, wg_ref[...], preferred_element_type=F32)
        up_v[...] = jnp.dot(h, wv_ref[...], preferred_element_type=F32)
        o_ref[...] += jnp.dot(act_down[...], w2_ref[...], preferred_element_type=F32)

    _by_parity(body, (g0_ref, v0_ref, a0_ref), (g1_ref, v1_ref, a1_ref))
    _mlp_epilogue(o_ref, fnw_ref, nch=nch, final_norm=final_norm)


def _mlp_index_maps(nch, n_tiles):
    up_tile = lambda s: jnp.minimum(s // nch, n_tiles - 1)
    up_chunk = lambda s: s % nch
    ew_chunk = lambda s: jnp.maximum(s - 1, 0) % nch
    down_chunk = lambda s: jnp.maximum(s - 2, 0) % nch
    down_tile = lambda s: jnp.maximum(s - 2, 0) // nch
    return up_tile, up_chunk, ew_chunk, down_chunk, down_tile


def _conv_ffn(x, nw, w_up, conv_w, conv_b, w_down, fnw, *, seq, tm, tc, final_norm):
    t, d = x.shape
    d_ff = w_down.shape[0]
    nch = d_ff // tc
    n_tiles = t // tm
    assert nch >= 3 and tc % min(COL_BLOCK, tc) == 0 and tm % (2 * ROW_TILE) == 0
    up_tile, up_chunk, ew_chunk, down_chunk, down_tile = _mlp_index_maps(nch, n_tiles)
    kern = functools.partial(_ffn_kernel, nch=nch, n_pairs=n_tiles * nch, tiles_per_seq=seq // tm,
                             final_norm=final_norm)
    taps = _tap_rows(conv_w)
    bias = jnp.broadcast_to(conv_b, (ROW_TILE, conv_b.shape[1]))
    n_tap_rows = FFN_CONV * ROW_TILE
    return pl.pallas_call(
        kern,
        grid=(n_tiles * nch + 2,),
        in_specs=[
            pl.BlockSpec((tm, d), lambda s: (up_tile(s), 0)),
            pl.BlockSpec((1, d), lambda s: (0, 0)),
            pl.BlockSpec((d, tc), lambda s: (0, up_chunk(s))),
            pl.BlockSpec((d, tc), lambda s: (0, nch + up_chunk(s))),
            pl.BlockSpec((n_tap_rows, tc), lambda s: (0, ew_chunk(s))),
            pl.BlockSpec((n_tap_rows, tc), lambda s: (0, nch + ew_chunk(s))),
            pl.BlockSpec((ROW_TILE, tc), lambda s: (0, ew_chunk(s))),
            pl.BlockSpec((ROW_TILE, tc), lambda s: (0, nch + ew_chunk(s))),
            pl.BlockSpec((tc, d), lambda s: (down_chunk(s), 0)),
            pl.BlockSpec((1, d), lambda s: (0, 0)),
        ],
        out_specs=pl.BlockSpec((tm, d), lambda s: (down_tile(s), 0)),
        out_shape=jax.ShapeDtypeStruct((t, d), F32),
        scratch_shapes=[
            pltpu.VMEM((tm, d), BF16),
            pltpu.VMEM((tm, tc), F32), pltpu.VMEM((tm, tc), F32), pltpu.VMEM((tm, tc), BF16),
            pltpu.VMEM((tm, tc), F32), pltpu.VMEM((tm, tc), F32), pltpu.VMEM((tm, tc), BF16),
            pltpu.VMEM((nch, ROW_TILE, tc), F32),
            pltpu.VMEM((nch, ROW_TILE, tc), F32),
        ],
        compiler_params=pltpu.CompilerParams(
            dimension_semantics=("arbitrary",),
            vmem_limit_bytes=VMEM_LIMIT_BYTES),
        name="conv_ffn",
    )(x, nw, w_up, w_up, taps, taps, bias, bias, w_down, fnw)


def _sc_elementwise(b_ref, c_ref, hid_ref, act_ref, cw_ref, uhalo_ref, chunk, seq_start):
    tm, tc = b_ref.shape
    for c0 in range(0, tc, min(COL_BLOCK, tc)):
        cols = slice(c0, c0 + min(COL_BLOCK, tc))
        conv = _ConvTaps(_tap_tiles(cw_ref, 0, SC_WIDTH, cols), _history(uhalo_ref, chunk, cols, seq_start))
        u = None
        for r0 in range(0, tm, 2 * ROW_TILE):
            halves = []
            for r in (r0, r0 + ROW_TILE):
                u = c_ref[r:r + ROW_TILE, cols] * hid_ref[r:r + ROW_TILE, cols]
                halves.append(b_ref[r:r + ROW_TILE, cols] * conv(u))
            act_ref[r0:r0 + 2 * ROW_TILE, cols] = jnp.concatenate(halves, axis=0).astype(BF16)
        uhalo_ref[chunk, :, cols] = u


def _sc_kernel(x_ref, nw_ref, wb_ref, wc_ref, wh_ref, cw_ref, w2_ref,
               o_ref, h_ref, b0_ref, c0_ref, hid0_ref, a0_ref, b1_ref, c1_ref, hid1_ref, a1_ref, uhalo_ref,
               *, nch, n_pairs, tiles_per_seq):
    chunk, seq_start = _elementwise_stage_ids(nch, tiles_per_seq)
    _mlp_prologue(x_ref, nw_ref, o_ref, h_ref,
                  (b0_ref, c0_ref, hid0_ref, a0_ref, b1_ref, c1_ref, hid1_ref, a1_ref, uhalo_ref),
                  nch=nch, n_pairs=n_pairs)

    def body(cur, prv):
        up_b, up_c, up_hid, act_down = cur
        ew_b, ew_c, ew_hid, act_ew = prv
        _sc_elementwise(ew_b, ew_c, ew_hid, act_ew, cw_ref, uhalo_ref, chunk, seq_start)
        h = h_ref[...]
        up_b[...] = jnp.dot(h, wb_ref[...], preferred_element_type=F32)
        up_c[...] = jnp.dot(h, wc_ref[...], preferred_element_type=F32)
        up_hid[...] = jnp.dot(h, wh_ref[...], preferred_element_type=F32)
        o_ref[...] += jnp.dot(act_down[...], w2_ref[...], preferred_element_type=F32)

    _by_parity(body, (b0_ref, c0_ref, hid0_ref, a0_ref), (b1_ref, c1_ref, hid1_ref, a1_ref))


def _short_conv(x, nw, w_in, conv_w, w_out, *, seq, tm, tc):
    t, d = x.shape
    sc_dim = w_out.shape[0]
    nch = sc_dim // tc
    n_tiles = t // tm
    assert nch >= 3 and tc % min(COL_BLOCK, tc) == 0 and tm % (2 * ROW_TILE) == 0
    up_tile, up_chunk, ew_chunk, down_chunk, down_tile = _mlp_index_maps(nch, n_tiles)
    kern = functools.partial(_sc_kernel, nch=nch, n_pairs=n_tiles * nch, tiles_per_seq=seq // tm)
    up_buf = pltpu.VMEM((tm, tc), F32)
    return pl.pallas_call(
        kern,
        grid=(n_tiles * nch + 2,),
        in_specs=[
            pl.BlockSpec((tm, d), lambda s: (up_tile(s), 0)),
            pl.BlockSpec((1, d), lambda s: (0, 0)),
            pl.BlockSpec((d, tc), lambda s: (0, up_chunk(s))),
            pl.BlockSpec((d, tc), lambda s: (0, nch + up_chunk(s))),
            pl.BlockSpec((d, tc), lambda s: (0, 2 * nch + up_chunk(s))),
            pl.BlockSpec((SC_WIDTH * ROW_TILE, tc), lambda s: (0, ew_chunk(s))),
            pl.BlockSpec((tc, d), lambda s: (down_chunk(s), 0)),
        ],
        out_specs=pl.BlockSpec((tm, d), lambda s: (down_tile(s), 0)),
        out_shape=jax.ShapeDtypeStruct((t, d), F32),
        scratch_shapes=[
            pltpu.VMEM((tm, d), BF16),
            up_buf, up_buf, up_buf, pltpu.VMEM((tm, tc), BF16),
            up_buf, up_buf, up_buf, pltpu.VMEM((tm, tc), BF16),
            pltpu.VMEM((nch, ROW_TILE, tc), F32),
        ],
        compiler_params=pltpu.CompilerParams(
            dimension_semantics=("arbitrary",),
            vmem_limit_bytes=VMEM_LIMIT_BYTES),
        name="short_conv_mixer",
    )(x, nw, w_in, w_in, w_in, _tap_rows(conv_w), w_out)


def _tile(n, pref):
    c = min(pref, n)
    while n % c:
        c -= LANES
    return c


def _trunk(x, mix_norm_w, ffn_norm_w, final_norm_w,
           ssd_w_in, ssd_conv_w, ssd_conv_b, ssd_dt_bias, ssd_a_log, ssd_d, ssd_norm_w, ssd_w_out,
           sc_w_in, sc_conv_w, sc_w_out,
           ffn_w_up, ffn_conv_w, ffn_conv_b, ffn_w_down,
           *, head_dim=SSM_HEAD_DIM, n_groups=SSM_GROUPS, n_state=SSM_STATE,
           tm_proj=1024, tn_proj=1024, tm_mlp=512, tc_mlp=512):
    batch, seq, d_model = x.shape
    depth = mix_norm_w.shape[0]
    t = batch * seq
    d_inner = ssd_w_out.shape[1]
    n_heads = ssd_dt_bias.shape[1]
    main_cols = ssd_w_in.shape[2] - n_heads
    assert d_inner == n_heads * head_dim

    tm_proj = _tile(seq, tm_proj)
    tm_mlp = _tile(seq, tm_mlp)
    xs = x.reshape(t, d_model)
    for i in range(depth):
        j = i // 2
        nw = mix_norm_w[i][None, :]
        if i % 2 == 0:
            w_main = ssd_w_in[j, :, :main_cols].astype(BF16)
            w_dt = jnp.pad(ssd_w_in[j, :, main_cols:], ((0, 0), (0, LANES - n_heads))).astype(BF16)
            zxbc, dt_raw = _norm_inproj(xs, nw, w_main, w_dt, tm=tm_proj, tn=_tile(main_cols, tn_proj))
            y = _ssd_core(zxbc, dt_raw, ssd_conv_w[j], ssd_conv_b[j], ssd_dt_bias[j], ssd_a_log[j],
                          ssd_d[j], ssd_norm_w[j], batch=batch, seq=seq, d_inner=d_inner,
                          n_groups=n_groups, n_state=n_state, head_dim=head_dim)
            xs = _proj_residual(xs, y, ssd_w_out[j].astype(BF16), tm=tm_mlp, tn=_tile(d_model, tn_proj))
        else:
            xs = _short_conv(xs, nw, sc_w_in[j].astype(BF16), sc_conv_w[j], sc_w_out[j].astype(BF16),
                             seq=seq, tm=tm_mlp, tc=_tile(sc_w_out.shape[1], tc_mlp))
        xs = _conv_ffn(xs, ffn_norm_w[i][None, :], ffn_w_up[i].astype(BF16), ffn_conv_w[i],
                       ffn_conv_b[i][None, :], ffn_w_down[i].astype(BF16), final_norm_w[None, :],
                       seq=seq, tm=tm_mlp, tc=_tile(ffn_w_down.shape[1], tc_mlp),
                       final_norm=(i == depth - 1))
    return xs.reshape(batch, seq, d_model)


def kernel(x, mix_norm_w, ffn_norm_w, final_norm_w, ssd_w_in, ssd_conv_w, ssd_conv_b, ssd_dt_bias,
           ssd_a_log, ssd_d, ssd_norm_w, ssd_w_out, sc_w_in, sc_conv_w, sc_w_out,
           ffn_w_up, ffn_conv_w, ffn_conv_b, ffn_w_down):
    return _trunk(x, mix_norm_w, ffn_norm_w, final_norm_w,
                  ssd_w_in, ssd_conv_w, ssd_conv_b, ssd_dt_bias, ssd_a_log, ssd_d, ssd_norm_w, ssd_w_out,
                  sc_w_in, sc_conv_w, sc_w_out,
                  ffn_w_up, ffn_conv_w, ffn_conv_b, ffn_w_down)
```

```python
import functools

import jax
import jax.numpy as jnp
from jax import lax
from jax.experimental import pallas as pl
from jax.experimental.pallas import tpu as pltpu

F32 = jnp.float32
BF16 = jnp.bfloat16

EPS = 1e-5
SSM_HEAD_DIM = 64
SSM_GROUPS = 8
SSM_STATE = 128
SSM_CHUNK = 128
SSM_CONV = 4
SC_WIDTH = 3
FFN_CONV = 3

LANES = 128
SUBLANES = 8
VMEM_LIMIT_BYTES = 56 * 1024 * 1024

NEG_LOG2_E = -1.4426950408889634


def _sigmoid(v):
    return 1.0 / (1.0 + jnp.exp2(v * NEG_LOG2_E))


def _rmsnorm_rows(x_ref, w_ref, out_ref, rows_per_step=64):
    n_rows = x_ref.shape[0]
    w = w_ref[...]

    def body(r, carry):
        sl = pl.ds(pl.multiple_of(r * rows_per_step, rows_per_step), rows_per_step)
        x = x_ref[sl, :]
        ms = jnp.mean(x * x, axis=-1, keepdims=True)
        out_ref[sl, :] = (x * lax.rsqrt(ms + EPS) * w).astype(out_ref.dtype)
        return carry

    lax.fori_loop(0, n_rows // rows_per_step, body, 0)


ROW_TILE = SUBLANES
COL_BLOCK = 2 * LANES


class _ConvTaps:
    def __init__(self, taps, history):
        self.taps = taps
        shifts = range(1, len(taps))
        row = lax.broadcasted_iota(jnp.int32, history.shape, 0)
        self.keep = [row >= j for j in shifts]
        self.rolled = [pltpu.roll(history, j, 0) for j in shifts]

    def __call__(self, u):
        n = len(self.taps)
        rolled = [pltpu.roll(u, j, 0) for j in range(1, n)]
        acc = self.taps[n - 1] * u
        for j in range(1, n):
            acc = acc + self.taps[n - 1 - j] * jnp.where(self.keep[j - 1], rolled[j - 1], self.rolled[j - 1])
        self.rolled = rolled
        return acc


def _tap_tiles(par_ref, row0, n_taps, cols):
    return [par_ref[row0 + ROW_TILE * k:row0 + ROW_TILE * (k + 1), cols] for k in range(n_taps)]


def _tap_rows(conv_w):
    return jnp.repeat(conv_w, ROW_TILE, axis=0)


def _norm_inproj_kernel(x_ref, nw_ref, w_ref, wdt_ref, o_ref, dt_ref, h_ref):
    @pl.when(pl.program_id(1) == 0)
    def _():
        _rmsnorm_rows(x_ref, nw_ref, h_ref)
        dt_ref[...] = jnp.dot(h_ref[...], wdt_ref[...], preferred_element_type=F32)

    o_ref[...] = jnp.dot(h_ref[...], w_ref[...], preferred_element_type=F32).astype(o_ref.dtype)


def _norm_inproj(x, nw, w, wdt, *, tm, tn):
    t, d = x.shape
    n = w.shape[1]
    return pl.pallas_call(
        _norm_inproj_kernel,
        grid=(t // tm, n // tn),
        in_specs=[
            pl.BlockSpec((tm, d), lambda i, j: (i, 0)),
            pl.BlockSpec((1, d), lambda i, j: (0, 0)),
            pl.BlockSpec((d, tn), lambda i, j: (0, j)),
            pl.BlockSpec((d, LANES), lambda i, j: (0, 0)),
        ],
        out_specs=[
            pl.BlockSpec((tm, tn), lambda i, j: (i, j)),
            pl.BlockSpec((tm, LANES), lambda i, j: (i, 0)),
        ],
        out_shape=[
            jax.ShapeDtypeStruct((t, n), BF16),
            jax.ShapeDtypeStruct((t, LANES), F32),
        ],
        scratch_shapes=[pltpu.VMEM((tm, d), BF16)],
        compiler_params=pltpu.CompilerParams(
            dimension_semantics=("arbitrary", "arbitrary"),
            vmem_limit_bytes=VMEM_LIMIT_BYTES),
        name="ssd_norm_inproj",
    )(x, nw, w, wdt)


def _split_bf16(v, parts):
    out = []
    rem = v
    for _ in range(parts):
        p = rem.astype(BF16)
        out.append(p)
        rem = rem - p.astype(F32)
    return out


def _ssd_core_kernel(z_ref, x_ref, bc_ref, dt_ref, cpar_ref,
                     dtb_ref, alog_ref, dexp_ref, nw_ref, expand_ref, o_ref,
                     state_ref, hist_ref, xc_ref, bcc_ref, xdt_ref, xw_ref, ex_ref,
                     *, n_groups, heads_per_group, head_dim, n_state):
    L = SSM_CHUNK
    gw = heads_per_group * head_dim
    bdim = n_groups * n_state

    @pl.when(pl.program_id(1) == 0)
    def _():
        state_ref[...] = jnp.zeros_like(state_ref)
        hist_ref[...] = jnp.zeros_like(hist_ref)

    d_in = x_ref.shape[1]
    bias_row0 = SSM_CONV * ROW_TILE
    for src_ref, col0 in ((x_ref, 0), (bc_ref, d_in)):
        for c0 in range(0, src_ref.shape[1], COL_BLOCK):
            cols = slice(c0, c0 + COL_BLOCK)
            pcols = slice(col0 + c0, col0 + c0 + COL_BLOCK)
            conv = _ConvTaps(_tap_tiles(cpar_ref, 0, SSM_CONV, pcols), hist_ref[:, pcols])
            bias = cpar_ref[bias_row0:bias_row0 + ROW_TILE, pcols]
            tile = None
            for r0 in range(0, L, 2 * ROW_TILE):
                blk = src_ref[r0:r0 + 2 * ROW_TILE, cols].astype(F32)
                halves = []
                for tile in (blk[0:ROW_TILE], blk[ROW_TILE:2 * ROW_TILE]):
                    a = conv(tile) + bias
                    halves.append(a * _sigmoid(a))
                if src_ref is x_ref:
                    xc_ref[r0:r0 + 2 * ROW_TILE, cols] = jnp.concatenate(halves, axis=0)
                else:
                    bcc_ref[r0:r0 + 2 * ROW_TILE, cols] = jnp.concatenate(halves, axis=0).astype(BF16)
            hist_ref[:, pcols] = tile

    dtr = dt_ref[...] + dtb_ref[...]
    dtv = jnp.maximum(dtr, 0.0) + jnp.log(1.0 + jnp.exp(-jnp.abs(dtr)))
    a = -jnp.exp(alog_ref[...])
    da = dtv * a

    row = lax.broadcasted_iota(jnp.int32, (L, L), 0)
    col = lax.broadcasted_iota(jnp.int32, (L, L), 1)
    causal = row >= col
    tril = jnp.where(causal, 1.0, 0.0).astype(BF16)
    acs = jnp.dot(jnp.concatenate([tril, tril, tril], axis=1),
                  jnp.concatenate(_split_bf16(da, 3), axis=0),
                  preferred_element_type=F32)
    acs_t = acs.T
    last = acs[L - 1:L, :]
    dte = jnp.exp(last - acs)
    cdec = jnp.broadcast_to(jnp.exp(last), (2 * SUBLANES, LANES))

    q = jnp.concatenate([dtv, dtv * dte, cdec], axis=0)
    ex_ref[...] = jnp.dot(jnp.concatenate(_split_bf16(q, 2), axis=1), expand_ref[...],
                          preferred_element_type=F32)
    xc = xc_ref[...]
    xdt_ref[...] = (xc * ex_ref[0:L, :]).astype(BF16)
    xw_ref[...] = (xc * ex_ref[L:2 * L, :]).astype(BF16)

    lane = lax.broadcasted_iota(jnp.int32, (L, LANES), 1)
    lo_half = lane < head_dim
    heads_per_tile = LANES // head_dim
    tiles_per_group = gw // LANES
    neg_inf = jnp.float32(-jnp.inf)

    for g in range(n_groups):
        gs = slice(g * gw, (g + 1) * gw)
        bg = bcc_ref[:, g * n_state:(g + 1) * n_state]
        cg = bcc_ref[:, bdim + g * n_state:bdim + (g + 1) * n_state]
        cb = lax.dot_general(cg, bg, (((1,), (1,)), ((), ())), preferred_element_type=F32)
        cg32 = cg.astype(F32)
        prev = state_ref[:, gs]
        prev_bf = prev.astype(BF16)
        y_tiles = []
        for tl in range(tiles_per_group):
            lhs_parts = []
            rhs_parts = []
            xp = xdt_ref[:, g * gw + tl * LANES:g * gw + (tl + 1) * LANES]
            pp = prev_bf[:, tl * LANES:(tl + 1) * LANES]
            for hh in range(heads_per_tile):
                h = g * heads_per_group + tl * heads_per_tile + hh
                acol = acs[:, h:h + 1]
                arow = acs_t[h:h + 1, :]
                decay = jnp.exp(jnp.where(causal, acol - arow, neg_inf))
                lhs_parts.append((cb * decay).astype(BF16))
                lhs_parts.append((cg32 * jnp.exp(acol)).astype(BF16))
                keep = lo_half if hh == 0 else jnp.logical_not(lo_half)
                rhs_parts.append(jnp.where(keep, xp, jnp.zeros_like(xp)))
                rhs_parts.append(jnp.where(keep, pp, jnp.zeros_like(pp)))
            y_tiles.append(jnp.dot(jnp.concatenate(lhs_parts, axis=1),
                                   jnp.concatenate(rhs_parts, axis=0),
                                   preferred_element_type=F32))
        y = jnp.concatenate(y_tiles, axis=1)

        s_new = lax.dot_general(bg, xw_ref[:, gs], (((0,), (0,)), ((), ())),
                                preferred_element_type=F32)
        state_ref[:, gs] = prev * ex_ref[2 * L:2 * L + 1, gs] + s_new

        yv = y + xc_ref[:, gs] * dexp_ref[:, gs]
        zg = z_ref[:, gs].astype(F32)
        yv = yv * (zg * _sigmoid(zg))
        ms = jnp.mean(yv * yv, axis=-1, keepdims=True)
        o_ref[:, gs] = (yv * lax.rsqrt(ms + EPS) * nw_ref[:, gs]).astype(o_ref.dtype)


def _ssd_core(zxbc, dt_raw, conv_w, conv_b, dt_bias, a_log, d_skip, norm_w, *,
              batch, seq, d_inner, n_groups, n_state, head_dim):
    t = zxbc.shape[0]
    n_heads = d_inner // head_dim
    heads_per_group = n_heads // n_groups
    bdim = n_groups * n_state
    L = SSM_CHUNK
    chunks = seq // L
    assert d_inner % (2 * bdim) == 0 and LANES % head_dim == 0 and n_heads <= LANES
    assert d_inner % COL_BLOCK == 0 and (2 * bdim) % COL_BLOCK == 0

    conv_dim = conv_w.shape[1]
    cpar = jnp.concatenate([_tap_rows(conv_w), jnp.broadcast_to(conv_b[None, :], (ROW_TILE, conv_dim))], axis=0)
    pad = LANES - n_heads
    dtb = jnp.pad(dt_bias, (0, pad))[None, :]
    alog = jnp.pad(a_log, (0, pad))[None, :]
    dexp = jnp.repeat(d_skip, head_dim)[None, :]
    nw = norm_w[None, :]
    e = (jnp.arange(LANES)[:, None] == (jnp.arange(d_inner) // head_dim)[None, :]).astype(BF16)
    expand = jnp.concatenate([e, e], axis=0)

    row_map = lambda b, c: (b * chunks + c, 0)
    const = lambda b, c: (0, 0)
    kern = functools.partial(_ssd_core_kernel, n_groups=n_groups, heads_per_group=heads_per_group,
                             head_dim=head_dim, n_state=n_state)
    return pl.pallas_call(
        kern,
        grid=(batch, chunks),
        in_specs=[
            pl.BlockSpec((L, d_inner), row_map),
            pl.BlockSpec((L, d_inner), lambda b, c: (b * chunks + c, 1)),
            pl.BlockSpec((L, 2 * bdim), lambda b, c: (b * chunks + c, d_inner // bdim)),
            pl.BlockSpec((L, LANES), row_map),
            pl.BlockSpec(((SSM_CONV + 1) * ROW_TILE, conv_dim), const),
            pl.BlockSpec((1, LANES), const),
            pl.BlockSpec((1, LANES), const),
            pl.BlockSpec((1, d_inner), const),
            pl.BlockSpec((1, d_inner), const),
            pl.BlockSpec((2 * LANES, d_inner), const),
        ],
        out_specs=pl.BlockSpec((L, d_inner), row_map),
        out_shape=jax.ShapeDtypeStruct((t, d_inner), BF16),
        scratch_shapes=[
            pltpu.VMEM((n_state, d_inner), F32),
            pltpu.VMEM((ROW_TILE, conv_dim), F32),
            pltpu.VMEM((L, d_inner), F32),
            pltpu.VMEM((L, 2 * bdim), BF16),
            pltpu.VMEM((L, d_inner), BF16),
            pltpu.VMEM((L, d_inner), BF16),
            pltpu.VMEM((2 * L + 2 * SUBLANES, d_inner), F32),
        ],
        compiler_params=pltpu.CompilerParams(
            dimension_semantics=("arbitrary", "arbitrary"),
            vmem_limit_bytes=VMEM_LIMIT_BYTES),
        name="ssd_core",
    )(zxbc, zxbc, zxbc, dt_raw, cpar, dtb, alog, dexp, nw, expand)


def _proj_residual_kernel(x_ref, y_ref, w_ref, o_ref):
    o_ref[...] = x_ref[...] + jnp.dot(y_ref[...], w_ref[...], preferred_element_type=F32)


def _proj_residual(x, y, w, *, tm, tn):
    t, d = x.shape
    k = y.shape[1]
    return pl.pallas_call(
        _proj_residual_kernel,
        grid=(d // tn, t // tm),
        in_specs=[
            pl.BlockSpec((tm, tn), lambda j, i: (i, j)),
            pl.BlockSpec((tm, k), lambda j, i: (i, 0)),
            pl.BlockSpec((k, tn), lambda j, i: (0, j)),
        ],
        out_specs=pl.BlockSpec((tm, tn), lambda j, i: (i, j)),
        out_shape=jax.ShapeDtypeStruct((t, d), F32),
        compiler_params=pltpu.CompilerParams(
            dimension_semantics=("arbitrary", "arbitrary"),
            vmem_limit_bytes=VMEM_LIMIT_BYTES),
        name="ssd_outproj_residual",
    )(x, y, w)


def _mlp_prologue(x_ref, nw_ref, o_ref, h_ref, zero_refs, *, nch, n_pairs):
    s = pl.program_id(0)

    @pl.when(s == 0)
    def _():
        for r in zero_refs:
            r[...] = jnp.zeros(r.shape, r.dtype)
        o_ref[...] = jnp.zeros(o_ref.shape, o_ref.dtype)

    @pl.when(jnp.logical_and(s % nch == 0, s < n_pairs))
    def _():
        _rmsnorm_rows(x_ref, nw_ref, h_ref)

    @pl.when(jnp.logical_and((s - 2) % nch == 0, s >= 2))
    def _():
        o_ref[...] = x_ref[...]


def _mlp_epilogue(o_ref, fnw_ref, *, nch, final_norm):
    if final_norm:
        s = pl.program_id(0)

        @pl.when(jnp.logical_and((s - 2) % nch == nch - 1, s >= 2))
        def _():
            _rmsnorm_rows(o_ref, fnw_ref, o_ref)


def _by_parity(body, set_a, set_b):
    s = pl.program_id(0)

    @pl.when(s % 2 == 0)
    def _():
        body(set_a, set_b)

    @pl.when(s % 2 == 1)
    def _():
        body(set_b, set_a)


def _elementwise_stage_ids(nch, tiles_per_seq):
    pair = jnp.maximum(pl.program_id(0) - 1, 0)
    return pair % nch, (pair // nch) % tiles_per_seq == 0


def _history(halo_ref, chunk, cols, seq_start):
    hist = halo_ref[chunk, :, cols]
    return jnp.where(seq_start, jnp.zeros_like(hist), hist)


def _zero_after(v):
    return jnp.minimum(jnp.abs(v), 0.0)


def _ffn_elementwise(g_ref, v_ref, act_ref, cwg_ref, cwv_ref, cbg_ref, cbv_ref, ghalo_ref, vhalo_ref,
                     chunk, seq_start):
    tm, tc = g_ref.shape
    wait = None
    for c0 in range(0, tc, min(COL_BLOCK, tc)):
        cols = slice(c0, c0 + min(COL_BLOCK, tc))
        conv_g = _ConvTaps(_tap_tiles(cwg_ref, 0, FFN_CONV, cols), _history(ghalo_ref, chunk, cols, seq_start))
        conv_v = _ConvTaps(_tap_tiles(cwv_ref, 0, FFN_CONV, cols), _history(vhalo_ref, chunk, cols, seq_start))
        bias_g = cbg_ref[:, cols]
        bias_v = cbv_ref[:, cols]
        for r0 in range(0, tm, 2 * ROW_TILE):
            halves = []
            for r in (r0, r0 + ROW_TILE):
                gt = g_ref[r:r + ROW_TILE, cols]
                vt = v_ref[r:r + ROW_TILE, cols]
                if wait is not None:
                    gt = gt + wait
                    vt = vt + wait
                g = conv_g(gt) + bias_g
                v = conv_v(vt) + bias_v
                den = 1.0 + jnp.exp2(g * NEG_LOG2_E)
                wait = _zero_after(den)
                halves.append(g * (1.0 / den) * v)
            act_ref[r0:r0 + 2 * ROW_TILE, cols] = jnp.concatenate(halves, axis=0).astype(BF16)
        ghalo_ref[chunk, :, cols] = g_ref[tm - ROW_TILE:tm, cols]
        vhalo_ref[chunk, :, cols] = v_ref[tm - ROW_TILE:tm, cols]


def _ffn_kernel(x_ref, nw_ref, wg_ref, wv_ref, cwg_ref, cwv_ref, cbg_ref, cbv_ref, w2_ref, fnw_ref,
                o_ref, h_ref, g0_ref, v0_ref, a0_ref, g1_ref, v1_ref, a1_ref, ghalo_ref, vhalo_ref,
                *, nch, n_pairs, tiles_per_seq, final_norm):
    chunk, seq_start = _elementwise_stage_ids(nch, tiles_per_seq)
    _mlp_prologue(x_ref, nw_ref, o_ref, h_ref,
                  (g0_ref, v0_ref, a0_ref, g1_ref, v1_ref, a1_ref, ghalo_ref, vhalo_ref),
                  nch=nch, n_pairs=n_pairs)

    def body(cur, prv):
        up_g, up_v, act_down = cur
        ew_g, ew_v, act_ew = prv
        _ffn_elementwise(ew_g, ew_v, act_ew, cwg_ref, cwv_ref, cbg_ref, cbv_ref, ghalo_ref, vhalo_ref,
                         chunk, seq_start)
        h = h_ref[...]
        up_g[...] = jnp.dot(h, wg_ref[...], preferred_element_type=F32)
        up_v[...] = jnp.dot(h, wv_ref[...], preferred_element_type=F32)
        o_ref[...] += jnp.dot(act_down[...], w2_ref[...], preferred_element_type=F32)

    _by_parity(body, (g0_ref, v0_ref, a0_ref), (g1_ref, v1_ref, a1_ref))
    _mlp_epilogue(o_ref, fnw_ref, nch=nch, final_norm=final_norm)


def _mlp_index_maps(nch, n_tiles):
    up_tile = lambda s: jnp.minimum(s // nch, n_tiles - 1)
    up_chunk = lambda s: s % nch
    ew_chunk = lambda s: jnp.maximum(s - 1, 0) % nch
    down_chunk = lambda s: jnp.maximum(s - 2, 0) % nch
    down_tile = lambda s: jnp.maximum(s - 2, 0) // nch
    return up_tile, up_chunk, ew_chunk, down_chunk, down_tile


def _conv_ffn(x, nw, w_up, conv_w, conv_b, w_down, fnw, *, seq, tm, tc, final_norm):
    t, d = x.shape
    d_ff = w_down.shape[0]
    nch = d_ff // tc
    n_tiles = t // tm
    assert nch >= 3 and tc % min(COL_BLOCK, tc) == 0 and tm % (2 * ROW_TILE) == 0
    up_tile, up_chunk, ew_chunk, down_chunk, down_tile = _mlp_index_maps(nch, n_tiles)
    kern = functools.partial(_ffn_kernel, nch=nch, n_pairs=n_tiles * nch, tiles_per_seq=seq // tm,
                             final_norm=final_norm)
    taps = _tap_rows(conv_w)
    bias = jnp.broadcast_to(conv_b, (ROW_TILE, conv_b.shape[1]))
    n_tap_rows = FFN_CONV * ROW_TILE
    return pl.pallas_call(
        kern,
        grid=(n_tiles * nch + 2,),
        in_specs=[
            pl.BlockSpec((tm, d), lambda s: (up_tile(s), 0)),
            pl.BlockSpec((1, d), lambda s: (0, 0)),
            pl.BlockSpec((d, tc), lambda s: (0, up_chunk(s))),
            pl.BlockSpec((d, tc), lambda s: (0, nch + up_chunk(s))),
            pl.BlockSpec((n_tap_rows, tc), lambda s: (0, ew_chunk(s))),
            pl.BlockSpec((n_tap_rows, tc), lambda s: (0, nch + ew_chunk(s))),
            pl.BlockSpec((ROW_TILE, tc), lambda s: (0, ew_chunk(s))),
            pl.BlockSpec((ROW_TILE, tc), lambda s: (0, nch + ew_chunk(s))),
            pl.BlockSpec((tc, d), lambda s: (down_chunk(s), 0)),
            pl.BlockSpec((1, d), lambda s: (0, 0)),
        ],
        out_specs=pl.BlockSpec((tm, d), lambda s: (down_tile(s), 0)),
        out_shape=jax.ShapeDtypeStruct((t, d), F32),
        scratch_shapes=[
            pltpu.VMEM((tm, d), BF16),
            pltpu.VMEM((tm, tc), F32), pltpu.VMEM((tm, tc), F32), pltpu.VMEM((tm, tc), BF16),
            pltpu.VMEM((tm, tc), F32), pltpu.VMEM((tm, tc), F32), pltpu.VMEM((tm, tc), BF16),
            pltpu.VMEM((nch, ROW_TILE, tc), F32),
            pltpu.VMEM((nch, ROW_TILE, tc), F32),
        ],
        compiler_params=pltpu.CompilerParams(
            dimension_semantics=("arbitrary",),
            vmem_limit_bytes=VMEM_LIMIT_BYTES),
        name="conv_ffn",
    )(x, nw, w_up, w_up, taps, taps, bias, bias, w_down, fnw)


def _sc_elementwise(b_ref, c_ref, hid_ref, act_ref, cw_ref, uhalo_ref, chunk, seq_start):
    tm, tc = b_ref.shape
    wait = None
    for c0 in range(0, tc, min(COL_BLOCK, tc)):
        cols = slice(c0, c0 + min(COL_BLOCK, tc))
        conv = _ConvTaps(_tap_tiles(cw_ref, 0, SC_WIDTH, cols), _history(uhalo_ref, chunk, cols, seq_start))
        last = None
        for r0 in range(0, tm, 2 * ROW_TILE):
            halves = []
            for r in (r0, r0 + ROW_TILE):
                ct = c_ref[r:r + ROW_TILE, cols]
                if wait is not None:
                    ct = ct + wait
                last = c_ref[r:r + ROW_TILE, cols] * hid_ref[r:r + ROW_TILE, cols]
                cv = conv(ct * hid_ref[r:r + ROW_TILE, cols])
                wait = _zero_after(cv)
                halves.append(b_ref[r:r + ROW_TILE, cols] * cv)
            act_ref[r0:r0 + 2 * ROW_TILE, cols] = jnp.concatenate(halves, axis=0).astype(BF16)
        uhalo_ref[chunk, :, cols] = last


def _sc_kernel(x_ref, nw_ref, wb_ref, wc_ref, wh_ref, cw_ref, w2_ref,
               o_ref, h_ref, b0_ref, c0_ref, hid0_ref, a0_ref, b1_ref, c1_ref, hid1_ref, a1_ref, uhalo_ref,
               *, nch, n_pairs, tiles_per_seq):
    chunk, seq_start = _elementwise_stage_ids(nch, tiles_per_seq)
    _mlp_prologue(x_ref, nw_ref, o_ref, h_ref,
                  (b0_ref, c0_ref, hid0_ref, a0_ref, b1_ref, c1_ref, hid1_ref, a1_ref, uhalo_ref),
                  nch=nch, n_pairs=n_pairs)

    def body(cur, prv):
        up_b, up_c, up_hid, act_down = cur
        ew_b, ew_c, ew_hid, act_ew = prv
        _sc_elementwise(ew_b, ew_c, ew_hid, act_ew, cw_ref, uhalo_ref, chunk, seq_start)
        h = h_ref[...]
        up_b[...] = jnp.dot(h, wb_ref[...], preferred_element_type=F32)
        up_c[...] = jnp.dot(h, wc_ref[...], preferred_element_type=F32)
        up_hid[...] = jnp.dot(h, wh_ref[...], preferred_element_type=F32)
        o_ref[...] += jnp.dot(act_down[...], w2_ref[...], preferred_element_type=F32)

    _by_parity(body, (b0_ref, c0_ref, hid0_ref, a0_ref), (b1_ref, c1_ref, hid1_ref, a1_ref))


def _short_conv(x, nw, w_in, conv_w, w_out, *, seq, tm, tc):
    t, d = x.shape
    sc_dim = w_out.shape[0]
    nch = sc_dim // tc
    n_tiles = t // tm
    assert nch >= 3 and tc % min(COL_BLOCK, tc) == 0 and tm % (2 * ROW_TILE) == 0
    up_tile, up_chunk, ew_chunk, down_chunk, down_tile = _mlp_index_maps(nch, n_tiles)
    kern = functools.partial(_sc_kernel, nch=nch, n_pairs=n_tiles * nch, tiles_per_seq=seq // tm)
    up_buf = pltpu.VMEM((tm, tc), F32)
    return pl.pallas_call(
        kern,
        grid=(n_tiles * nch + 2,),
        in_specs=[
            pl.BlockSpec((tm, d), lambda s: (up_tile(s), 0)),
            pl.BlockSpec((1, d), lambda s: (0, 0)),
            pl.BlockSpec((d, tc), lambda s: (0, up_chunk(s))),
            pl.BlockSpec((d, tc), lambda s: (0, nch + up_chunk(s))),
            pl.BlockSpec((d, tc), lambda s: (0, 2 * nch + up_chunk(s))),
            pl.BlockSpec((SC_WIDTH * ROW_TILE, tc), lambda s: (0, ew_chunk(s))),
            pl.BlockSpec((tc, d), lambda s: (down_chunk(s), 0)),
        ],
        out_specs=pl.BlockSpec((tm, d), lambda s: (down_tile(s), 0)),
        out_shape=jax.ShapeDtypeStruct((t, d), F32),
        scratch_shapes=[
            pltpu.VMEM((tm, d), BF16),
            up_buf, up_buf, up_buf, pltpu.VMEM((tm, tc), BF16),
            up_buf, up_buf, up_buf, pltpu.VMEM((tm, tc), BF16),
            pltpu.VMEM((nch, ROW_TILE, tc), F32),
        ],
        compiler_params=pltpu.CompilerParams(
            dimension_semantics=("arbitrary",),
            vmem_limit_bytes=VMEM_LIMIT_BYTES),
        name="short_conv_mixer",
    )(x, nw, w_in, w_in, w_in, _tap_rows(conv_w), w_out)


def _tile(n, pref):
    c = min(pref, n)
    while n % c:
        c -= LANES
    return c


def _trunk(x, mix_norm_w, ffn_norm_w, final_norm_w,
           ssd_w_in, ssd_conv_w, ssd_conv_b, ssd_dt_bias, ssd_a_log, ssd_d, ssd_norm_w, ssd_w_out,
           sc_w_in, sc_conv_w, sc_w_out,
           ffn_w_up, ffn_conv_w, ffn_conv_b, ffn_w_down,
           *, head_dim=SSM_HEAD_DIM, n_groups=SSM_GROUPS, n_state=SSM_STATE,
           tm_proj=1024, tn_proj=1024, tm_mlp=512, tc_mlp=512):
    batch, seq, d_model = x.shape
    depth = mix_norm_w.shape[0]
    t = batch * seq
    d_inner = ssd_w_out.shape[1]
    n_heads = ssd_dt_bias.shape[1]
    main_cols = ssd_w_in.shape[2] - n_heads
    assert d_inner == n_heads * head_dim

    tm_proj = _tile(seq, tm_proj)
    tm_mlp = _tile(seq, tm_mlp)
    xs = x.reshape(t, d_model)
    for i in range(depth):
        j = i // 2
        nw = mix_norm_w[i][None, :]
        if i % 2 == 0:
            w_main = ssd_w_in[j, :, :main_cols].astype(BF16)
            w_dt = jnp.pad(ssd_w_in[j, :, main_cols:], ((0, 0), (0, LANES - n_heads))).astype(BF16)
            zxbc, dt_raw = _norm_inproj(xs, nw, w_main, w_dt, tm=tm_proj, tn=_tile(main_cols, tn_proj))
            y = _ssd_core(zxbc, dt_raw, ssd_conv_w[j], ssd_conv_b[j], ssd_dt_bias[j], ssd_a_log[j],
                          ssd_d[j], ssd_norm_w[j], batch=batch, seq=seq, d_inner=d_inner,
                          n_groups=n_groups, n_state=n_state, head_dim=head_dim)
            xs = _proj_residual(xs, y, ssd_w_out[j].astype(BF16), tm=tm_mlp, tn=_tile(d_model, tn_proj))
        else:
            xs = _short_conv(xs, nw, sc_w_in[j].astype(BF16), sc_conv_w[j], sc_w_out[j].astype(BF16),
                             seq=seq, tm=tm_mlp, tc=_tile(sc_w_out.shape[1], tc_mlp))
        xs = _conv_ffn(xs, ffn_norm_w[i][None, :], ffn_w_up[i].astype(BF16), ffn_conv_w[i],
                       ffn_conv_b[i][None, :], ffn_w_down[i].astype(BF16), final_norm_w[None, :],
                       seq=seq, tm=tm_mlp, tc=_tile(ffn_w_down.shape[1], tc_mlp),
                       final_norm=(i == depth - 1))
    return xs.reshape(batch, seq, d_model)


def kernel(x, mix_norm_w, ffn_norm_w, final_norm_w, ssd_w_in, ssd_conv_w, ssd_conv_b, ssd_dt_bias,
           ssd_a_log, ssd_d, ssd_norm_w, ssd_w_out, sc_w_in, sc_conv_w, sc_w_out,
           ffn_w_up, ffn_conv_w, ffn_conv_b, ffn_w_down):
    return _trunk(x, mix_norm_w, ffn_norm_w, final_norm_w,
                  ssd_w_in, ssd_conv_w, ssd_conv_b, ssd_dt_bias, ssd_a_log, ssd_d, ssd_norm_w, ssd_w_out,
                  sc_w_in, sc_conv_w, sc_w_out,
                  ffn_w_up, ffn_conv_w, ffn_conv_b, ffn_w_down)
```

```python
import functools

import jax
import jax.numpy as jnp
from jax import lax
from jax.experimental import pallas as pl
from jax.experimental.pallas import tpu as pltpu

F32 = jnp.float32
BF16 = jnp.bfloat16

EPS = 1e-5
SSM_HEAD_DIM = 64
SSM_GROUPS = 8
SSM_STATE = 128
SSM_CHUNK = 128
SSM_CONV = 4
SC_WIDTH = 3
FFN_CONV = 3

LANES = 128
SUBLANES = 8
VMEM_LIMIT_BYTES = 56 * 1024 * 1024

NEG_LOG2_E = -1.4426950408889634


def _sigmoid(v):
    return 1.0 / (1.0 + jnp.exp2(v * NEG_LOG2_E))


def _rmsnorm_rows(x_ref, w_ref, out_ref, rows_per_step=64):
    n_rows = x_ref.shape[0]
    w = w_ref[...]

    def body(r, carry):
        sl = pl.ds(pl.multiple_of(r * rows_per_step, rows_per_step), rows_per_step)
        x = x_ref[sl, :]
        ms = jnp.mean(x * x, axis=-1, keepdims=True)
        out_ref[sl, :] = (x * lax.rsqrt(ms + EPS) * w).astype(out_ref.dtype)
        return carry

    lax.fori_loop(0, n_rows // rows_per_step, body, 0)


ROW_TILE = SUBLANES
COL_BLOCK = 2 * LANES


class _ConvTaps:
    def __init__(self, taps, history):
        self.taps = taps
        shifts = range(1, len(taps))
        row = lax.broadcasted_iota(jnp.int32, history.shape, 0)
        self.keep = [row >= j for j in shifts]
        self.rolled = [pltpu.roll(history, j, 0) for j in shifts]

    def __call__(self, u):
        n = len(self.taps)
        rolled = [pltpu.roll(u, j, 0) for j in range(1, n)]
        acc = self.taps[n - 1] * u
        for j in range(1, n):
            acc = acc + self.taps[n - 1 - j] * jnp.where(self.keep[j - 1], rolled[j - 1], self.rolled[j - 1])
        self.rolled = rolled
        return acc


def _tap_tiles(par_ref, row0, n_taps, cols, chunk=None):
    lead = () if chunk is None else (chunk,)
    return [par_ref[lead + (slice(row0 + ROW_TILE * k, row0 + ROW_TILE * (k + 1)), cols)] for k in range(n_taps)]


def _tap_rows(conv_w):
    return jnp.repeat(conv_w, ROW_TILE, axis=0)


def _norm_inproj_kernel(x_ref, nw_ref, w_ref, wdt_ref, o_ref, dt_ref, h_ref):
    @pl.when(pl.program_id(1) == 0)
    def _():
        _rmsnorm_rows(x_ref, nw_ref, h_ref)
        dt_ref[...] = jnp.dot(h_ref[...], wdt_ref[...], preferred_element_type=F32)

    o_ref[...] = jnp.dot(h_ref[...], w_ref[...], preferred_element_type=F32).astype(o_ref.dtype)


def _norm_inproj(x, nw, w, wdt, *, tm, tn):
    t, d = x.shape
    n = w.shape[1]
    return pl.pallas_call(
        _norm_inproj_kernel,
        grid=(t // tm, n // tn),
        in_specs=[
            pl.BlockSpec((tm, d), lambda i, j: (i, 0)),
            pl.BlockSpec((1, d), lambda i, j: (0, 0)),
            pl.BlockSpec((d, tn), lambda i, j: (0, j)),
            pl.BlockSpec((d, LANES), lambda i, j: (0, 0)),
        ],
        out_specs=[
            pl.BlockSpec((tm, tn), lambda i, j: (i, j)),
            pl.BlockSpec((tm, LANES), lambda i, j: (i, 0)),
        ],
        out_shape=[
            jax.ShapeDtypeStruct((t, n), BF16),
            jax.ShapeDtypeStruct((t, LANES), F32),
        ],
        scratch_shapes=[pltpu.VMEM((tm, d), BF16)],
        compiler_params=pltpu.CompilerParams(
            dimension_semantics=("arbitrary", "arbitrary"),
            vmem_limit_bytes=VMEM_LIMIT_BYTES),
        name="ssd_norm_inproj",
    )(x, nw, w, wdt)


def _split_bf16(v, parts):
    out = []
    rem = v
    for _ in range(parts):
        p = rem.astype(BF16)
        out.append(p)
        rem = rem - p.astype(F32)
    return out


def _ssd_core_kernel(z_ref, x_ref, bc_ref, dt_ref, cpar_ref,
                     dtb_ref, alog_ref, dexp_ref, nw_ref, expand_ref, o_ref,
                     state_ref, hist_ref, xc_ref, bcc_ref, xdt_ref, xw_ref, ex_ref,
                     *, n_groups, heads_per_group, head_dim, n_state):
    L = SSM_CHUNK
    gw = heads_per_group * head_dim
    bdim = n_groups * n_state

    @pl.when(pl.program_id(1) == 0)
    def _():
        state_ref[...] = jnp.zeros_like(state_ref)
        hist_ref[...] = jnp.zeros_like(hist_ref)

    d_in = x_ref.shape[1]
    bias_row0 = SSM_CONV * ROW_TILE
    for src_ref, col0 in ((x_ref, 0), (bc_ref, d_in)):
        for c0 in range(0, src_ref.shape[1], COL_BLOCK):
            cols = slice(c0, c0 + COL_BLOCK)
            pcols = slice(col0 + c0, col0 + c0 + COL_BLOCK)
            conv = _ConvTaps(_tap_tiles(cpar_ref, 0, SSM_CONV, pcols), hist_ref[:, pcols])
            bias = cpar_ref[bias_row0:bias_row0 + ROW_TILE, pcols]
            tile = None
            for r0 in range(0, L, 2 * ROW_TILE):
                blk = src_ref[r0:r0 + 2 * ROW_TILE, cols].astype(F32)
                halves = []
                for tile in (blk[0:ROW_TILE], blk[ROW_TILE:2 * ROW_TILE]):
                    a = conv(tile) + bias
                    halves.append(a * _sigmoid(a))
                if src_ref is x_ref:
                    xc_ref[r0:r0 + 2 * ROW_TILE, cols] = jnp.concatenate(halves, axis=0)
                else:
                    bcc_ref[r0:r0 + 2 * ROW_TILE, cols] = jnp.concatenate(halves, axis=0).astype(BF16)
            hist_ref[:, pcols] = tile

    dtr = dt_ref[...] + dtb_ref[...]
    dtv = jnp.maximum(dtr, 0.0) + jnp.log(1.0 + jnp.exp(-jnp.abs(dtr)))
    a = -jnp.exp(alog_ref[...])
    da = dtv * a

    row = lax.broadcasted_iota(jnp.int32, (L, L), 0)
    col = lax.broadcasted_iota(jnp.int32, (L, L), 1)
    causal = row >= col
    tril = jnp.where(causal, 1.0, 0.0).astype(BF16)
    acs = jnp.dot(jnp.concatenate([tril, tril, tril], axis=1),
                  jnp.concatenate(_split_bf16(da, 3), axis=0),
                  preferred_element_type=F32)
    acs_t = acs.T
    last = acs[L - 1:L, :]
    dte = jnp.exp(last - acs)
    eacs = jnp.exp(acs)
    cdec = jnp.broadcast_to(jnp.exp(last), (2 * SUBLANES, LANES))

    q = jnp.concatenate([dtv, dtv * dte, cdec], axis=0)
    ex_ref[...] = jnp.dot(jnp.concatenate(_split_bf16(q, 2), axis=1), expand_ref[...],
                          preferred_element_type=F32)
    xc = xc_ref[...]
    xdt_ref[...] = (xc * ex_ref[0:L, :]).astype(BF16)
    xw_ref[...] = (xc * ex_ref[L:2 * L, :]).astype(BF16)

    lane = lax.broadcasted_iota(jnp.int32, (L, LANES), 1)
    lo_half = lane < head_dim
    heads_per_tile = LANES // head_dim
    tiles_per_group = gw // LANES
    neg_inf = jnp.float32(-jnp.inf)

    for g in range(n_groups):
        gs = slice(g * gw, (g + 1) * gw)
        bg = bcc_ref[:, g * n_state:(g + 1) * n_state]
        cg = bcc_ref[:, bdim + g * n_state:bdim + (g + 1) * n_state]
        cb = lax.dot_general(cg, bg, (((1,), (1,)), ((), ())), preferred_element_type=F32)
        cg32 = cg.astype(F32)
        prev = state_ref[:, gs]
        prev_bf = prev.astype(BF16)
        y_tiles = []
        for tl in range(tiles_per_group):
            lhs_parts = []
            rhs_parts = []
            xp = xdt_ref[:, g * gw + tl * LANES:g * gw + (tl + 1) * LANES]
            pp = prev_bf[:, tl * LANES:(tl + 1) * LANES]
            for hh in range(heads_per_tile):
                h = g * heads_per_group + tl * heads_per_tile + hh
                acol = acs[:, h:h + 1]
                arow = acs_t[h:h + 1, :]
                decay = jnp.exp(jnp.where(causal, acol - arow, neg_inf))
                lhs_parts.append((cb * decay).astype(BF16))
                lhs_parts.append((cg32 * eacs[:, h:h + 1]).astype(BF16))
                keep = lo_half if hh == 0 else jnp.logical_not(lo_half)
                rhs_parts.append(jnp.where(keep, xp, jnp.zeros_like(xp)))
                rhs_parts.append(jnp.where(keep, pp, jnp.zeros_like(pp)))
            y_tiles.append(jnp.dot(jnp.concatenate(lhs_parts, axis=1),
                                   jnp.concatenate(rhs_parts, axis=0),
                                   preferred_element_type=F32))
        y = jnp.concatenate(y_tiles, axis=1)

        s_new = lax.dot_general(bg, xw_ref[:, gs], (((0,), (0,)), ((), ())),
                                preferred_element_type=F32)
        state_ref[:, gs] = prev * ex_ref[2 * L:2 * L + 1, gs] + s_new

        yv = y + xc_ref[:, gs] * dexp_ref[:, gs]
        zg = z_ref[:, gs].astype(F32)
        yv = yv * (zg * _sigmoid(zg))
        ms = jnp.mean(yv * yv, axis=-1, keepdims=True)
        o_ref[:, gs] = (yv * lax.rsqrt(ms + EPS) * nw_ref[:, gs]).astype(o_ref.dtype)


def _ssd_core(zxbc, dt_raw, conv_w, conv_b, dt_bias, a_log, d_skip, norm_w, *,
              batch, seq, d_inner, n_groups, n_state, head_dim):
    t = zxbc.shape[0]
    n_heads = d_inner // head_dim
    heads_per_group = n_heads // n_groups
    bdim = n_groups * n_state
    L = SSM_CHUNK
    chunks = seq // L
    assert d_inner % (2 * bdim) == 0 and LANES % head_dim == 0 and n_heads <= LANES
    assert d_inner % COL_BLOCK == 0 and (2 * bdim) % COL_BLOCK == 0

    conv_dim = conv_w.shape[1]
    cpar = jnp.concatenate([_tap_rows(conv_w), jnp.broadcast_to(conv_b[None, :], (ROW_TILE, conv_dim))], axis=0)
    pad = LANES - n_heads
    dtb = jnp.pad(dt_bias, (0, pad))[None, :]
    alog = jnp.pad(a_log, (0, pad))[None, :]
    dexp = jnp.repeat(d_skip, head_dim)[None, :]
    nw = norm_w[None, :]
    e = (jnp.arange(LANES)[:, None] == (jnp.arange(d_inner) // head_dim)[None, :]).astype(BF16)
    expand = jnp.concatenate([e, e], axis=0)

    row_map = lambda b, c: (b * chunks + c, 0)
    const = lambda b, c: (0, 0)
    kern = functools.partial(_ssd_core_kernel, n_groups=n_groups, heads_per_group=heads_per_group,
                             head_dim=head_dim, n_state=n_state)
    return pl.pallas_call(
        kern,
        grid=(batch, chunks),
        in_specs=[
            pl.BlockSpec((L, d_inner), row_map),
            pl.BlockSpec((L, d_inner), lambda b, c: (b * chunks + c, 1)),
            pl.BlockSpec((L, 2 * bdim), lambda b, c: (b * chunks + c, d_inner // bdim)),
            pl.BlockSpec((L, LANES), row_map),
            pl.BlockSpec(((SSM_CONV + 1) * ROW_TILE, conv_dim), const),
            pl.BlockSpec((1, LANES), const),
            pl.BlockSpec((1, LANES), const),
            pl.BlockSpec((1, d_inner), const),
            pl.BlockSpec((1, d_inner), const),
            pl.BlockSpec((2 * LANES, d_inner), const),
        ],
        out_specs=pl.BlockSpec((L, d_inner), row_map),
        out_shape=jax.ShapeDtypeStruct((t, d_inner), BF16),
        scratch_shapes=[
            pltpu.VMEM((n_state, d_inner), F32),
            pltpu.VMEM((ROW_TILE, conv_dim), F32),
            pltpu.VMEM((L, d_inner), F32),
            pltpu.VMEM((L, 2 * bdim), BF16),
            pltpu.VMEM((L, d_inner), BF16),
            pltpu.VMEM((L, d_inner), BF16),
            pltpu.VMEM((2 * L + 2 * SUBLANES, d_inner), F32),
        ],
        compiler_params=pltpu.CompilerParams(
            dimension_semantics=("arbitrary", "arbitrary"),
            vmem_limit_bytes=VMEM_LIMIT_BYTES),
        name="ssd_core",
    )(zxbc, zxbc, zxbc, dt_raw, cpar, dtb, alog, dexp, nw, expand)


def _proj_residual_kernel(x_ref, y_ref, w_ref, o_ref):
    o_ref[...] = x_ref[...] + jnp.dot(y_ref[...], w_ref[...], preferred_element_type=F32)


def _proj_residual(x, y, w, *, tm, tn):
    t, d = x.shape
    k = y.shape[1]
    return pl.pallas_call(
        _proj_residual_kernel,
        grid=(d // tn, t // tm),
        in_specs=[
            pl.BlockSpec((tm, tn), lambda j, i: (i, j)),
            pl.BlockSpec((tm, k), lambda j, i: (i, 0)),
            pl.BlockSpec((k, tn), lambda j, i: (0, j)),
        ],
        out_specs=pl.BlockSpec((tm, tn), lambda j, i: (i, j)),
        out_shape=jax.ShapeDtypeStruct((t, d), F32),
        compiler_params=pltpu.CompilerParams(
            dimension_semantics=("arbitrary", "arbitrary"),
            vmem_limit_bytes=VMEM_LIMIT_BYTES),
        name="ssd_outproj_residual",
    )(x, y, w)


def _mlp_prologue(x_ref, nw_ref, o_ref, h_ref, zero_refs, *, nch, n_pairs):
    s = pl.program_id(0)

    @pl.when(s == 0)
    def _():
        for r in zero_refs:
            r[...] = jnp.zeros(r.shape, r.dtype)
        o_ref[...] = jnp.zeros(o_ref.shape, o_ref.dtype)

    @pl.when(jnp.logical_and(s % nch == 0, s < n_pairs))
    def _():
        _rmsnorm_rows(x_ref, nw_ref, h_ref)

    @pl.when(jnp.logical_and((s - 2) % nch == 0, s >= 2))
    def _():
        o_ref[...] = x_ref[...]


def _mlp_epilogue(o_ref, fnw_ref, *, nch, final_norm):
    if final_norm:
        s = pl.program_id(0)

        @pl.when(jnp.logical_and((s - 2) % nch == nch - 1, s >= 2))
        def _():
            _rmsnorm_rows(o_ref, fnw_ref, o_ref)


def _by_parity(body, set_a, set_b):
    s = pl.program_id(0)

    @pl.when(s % 2 == 0)
    def _():
        body(set_a, set_b)

    @pl.when(s % 2 == 1)
    def _():
        body(set_b, set_a)


def _elementwise_stage_ids(nch, tiles_per_seq):
    pair = jnp.maximum(pl.program_id(0) - 1, 0)
    return pair % nch, (pair // nch) % tiles_per_seq == 0


def _history(halo_ref, chunk, cols, seq_start):
    hist = halo_ref[chunk, :, cols]
    return jnp.where(seq_start, jnp.zeros_like(hist), hist)


def _zero_after(v):
    return jnp.minimum(jnp.abs(v), 0.0)


FFN_TAP_ROWS = FFN_CONV * ROW_TILE
FFN_PAR_ROWS = 2 * FFN_TAP_ROWS + 2 * ROW_TILE


def _ffn_elementwise(g_ref, v_ref, act_ref, par_ref, ghalo_ref, vhalo_ref, chunk, seq_start):
    tm, tc = g_ref.shape
    wait = None
    for c0 in range(0, tc, min(COL_BLOCK, tc)):
        cols = slice(c0, c0 + min(COL_BLOCK, tc))
        conv_g = _ConvTaps(_tap_tiles(par_ref, 0, FFN_CONV, cols, chunk),
                           _history(ghalo_ref, chunk, cols, seq_start))
        conv_v = _ConvTaps(_tap_tiles(par_ref, FFN_TAP_ROWS, FFN_CONV, cols, chunk),
                           _history(vhalo_ref, chunk, cols, seq_start))
        bias_g = par_ref[chunk, 2 * FFN_TAP_ROWS:2 * FFN_TAP_ROWS + ROW_TILE, cols]
        bias_v = par_ref[chunk, 2 * FFN_TAP_ROWS + ROW_TILE:FFN_PAR_ROWS, cols]
        for r0 in range(0, tm, 2 * ROW_TILE):
            halves = []
            for r in (r0, r0 + ROW_TILE):
                gt = g_ref[r:r + ROW_TILE, cols]
                vt = v_ref[r:r + ROW_TILE, cols]
                if wait is not None:
                    gt = gt + wait
                    vt = vt + wait
                g = conv_g(gt) + bias_g
                v = conv_v(vt) + bias_v
                den = 1.0 + jnp.exp2(g * NEG_LOG2_E)
                wait = _zero_after(den)
                halves.append(g * (1.0 / den) * v)
            act_ref[r0:r0 + 2 * ROW_TILE, cols] = jnp.concatenate(halves, axis=0).astype(BF16)
        ghalo_ref[chunk, :, cols] = g_ref[tm - ROW_TILE:tm, cols]
        vhalo_ref[chunk, :, cols] = v_ref[tm - ROW_TILE:tm, cols]


def _ffn_kernel(x_ref, nw_ref, wg_ref, wv_ref, par_ref, w2_ref, fnw_ref,
                o_ref, h_ref, g0_ref, v0_ref, a0_ref, g1_ref, v1_ref, a1_ref, ghalo_ref, vhalo_ref,
                *, nch, n_pairs, tiles_per_seq, final_norm):
    chunk, seq_start = _elementwise_stage_ids(nch, tiles_per_seq)
    _mlp_prologue(x_ref, nw_ref, o_ref, h_ref,
                  (g0_ref, v0_ref, a0_ref, g1_ref, v1_ref, a1_ref, ghalo_ref, vhalo_ref),
                  nch=nch, n_pairs=n_pairs)

    def body(cur, prv):
        up_g, up_v, act_down = cur
        ew_g, ew_v, act_ew = prv
        _ffn_elementwise(ew_g, ew_v, act_ew, par_ref, ghalo_ref, vhalo_ref, chunk, seq_start)
        h = h_ref[...]
        up_g[...] = jnp.dot(h, wg_ref[...], preferred_element_type=F32)
        up_v[...] = jnp.dot(h, wv_ref[...], preferred_element_type=F32)
        o_ref[...] += jnp.dot(act_down[...], w2_ref[...], preferred_element_type=F32)

    _by_parity(body, (g0_ref, v0_ref, a0_ref), (g1_ref, v1_ref, a1_ref))
    _mlp_epilogue(o_ref, fnw_ref, nch=nch, final_norm=final_norm)


def _mlp_index_maps(nch, n_tiles):
    up_tile = lambda s: jnp.minimum(s // nch, n_tiles - 1)
    up_chunk = lambda s: s % nch
    ew_chunk = lambda s: jnp.maximum(s - 1, 0) % nch
    down_chunk = lambda s: jnp.maximum(s - 2, 0) % nch
    down_tile = lambda s: jnp.maximum(s - 2, 0) // nch
    return up_tile, up_chunk, ew_chunk, down_chunk, down_tile


def _conv_ffn(x, nw, w_up, conv_w, conv_b, w_down, fnw, *, seq, tm, tc, final_norm):
    t, d = x.shape
    d_ff = w_down.shape[0]
    nch = d_ff // tc
    n_tiles = t // tm
    assert nch >= 3 and tc % min(COL_BLOCK, tc) == 0 and tm % (2 * ROW_TILE) == 0
    up_tile, up_chunk, ew_chunk, down_chunk, down_tile = _mlp_index_maps(nch, n_tiles)
    kern = functools.partial(_ffn_kernel, nch=nch, n_pairs=n_tiles * nch, tiles_per_seq=seq // tm,
                             final_norm=final_norm)
    taps = _tap_rows(conv_w).reshape(FFN_TAP_ROWS, 2 * nch, tc)
    bias = jnp.broadcast_to(conv_b, (ROW_TILE, conv_b.shape[1])).reshape(ROW_TILE, 2 * nch, tc)
    params = jnp.concatenate([taps[:, :nch], taps[:, nch:], bias[:, :nch], bias[:, nch:]], axis=0)
    params = params.transpose(1, 0, 2)
    return pl.pallas_call(
        kern,
        grid=(n_tiles * nch + 2,),
        in_specs=[
            pl.BlockSpec((tm, d), lambda s: (up_tile(s), 0)),
            pl.BlockSpec((1, d), lambda s: (0, 0)),
            pl.BlockSpec((d, tc), lambda s: (0, up_chunk(s))),
            pl.BlockSpec((d, tc), lambda s: (0, nch + up_chunk(s))),
            pl.BlockSpec((nch, FFN_PAR_ROWS, tc), lambda s: (0, 0, 0)),
            pl.BlockSpec((tc, d), lambda s: (down_chunk(s), 0)),
            pl.BlockSpec((1, d), lambda s: (0, 0)),
        ],
        out_specs=pl.BlockSpec((tm, d), lambda s: (down_tile(s), 0)),
        out_shape=jax.ShapeDtypeStruct((t, d), F32),
        scratch_shapes=[
            pltpu.VMEM((tm, d), BF16),
            pltpu.VMEM((tm, tc), F32), pltpu.VMEM((tm, tc), F32), pltpu.VMEM((tm, tc), BF16),
            pltpu.VMEM((tm, tc), F32), pltpu.VMEM((tm, tc), F32), pltpu.VMEM((tm, tc), BF16),
            pltpu.VMEM((nch, ROW_TILE, tc), F32),
            pltpu.VMEM((nch, ROW_TILE, tc), F32),
        ],
        compiler_params=pltpu.CompilerParams(
            dimension_semantics=("arbitrary",),
            vmem_limit_bytes=VMEM_LIMIT_BYTES),
        name="conv_ffn",
    )(x, nw, w_up, w_up, params, w_down, fnw)


def _sc_elementwise(b_ref, c_ref, hid_ref, act_ref, cw_ref, uhalo_ref, chunk, seq_start):
    tm, tc = b_ref.shape
    wait = None
    for c0 in range(0, tc, min(COL_BLOCK, tc)):
        cols = slice(c0, c0 + min(COL_BLOCK, tc))
        conv = _ConvTaps(_tap_tiles(cw_ref, 0, SC_WIDTH, cols, chunk), _history(uhalo_ref, chunk, cols, seq_start))
        last = None
        for r0 in range(0, tm, 2 * ROW_TILE):
            halves = []
            for r in (r0, r0 + ROW_TILE):
                ct = c_ref[r:r + ROW_TILE, cols]
                if wait is not None:
                    ct = ct + wait
                last = c_ref[r:r + ROW_TILE, cols] * hid_ref[r:r + ROW_TILE, cols]
                cv = conv(ct * hid_ref[r:r + ROW_TILE, cols])
                wait = _zero_after(cv)
                halves.append(b_ref[r:r + ROW_TILE, cols] * cv)
            act_ref[r0:r0 + 2 * ROW_TILE, cols] = jnp.concatenate(halves, axis=0).astype(BF16)
        uhalo_ref[chunk, :, cols] = last


def _sc_kernel(x_ref, nw_ref, wb_ref, wc_ref, wh_ref, cw_ref, w2_ref,
               o_ref, h_ref, b0_ref, c0_ref, hid0_ref, a0_ref, b1_ref, c1_ref, hid1_ref, a1_ref, uhalo_ref,
               *, nch, n_pairs, tiles_per_seq):
    chunk, seq_start = _elementwise_stage_ids(nch, tiles_per_seq)
    _mlp_prologue(x_ref, nw_ref, o_ref, h_ref,
                  (b0_ref, c0_ref, hid0_ref, a0_ref, b1_ref, c1_ref, hid1_ref, a1_ref, uhalo_ref),
                  nch=nch, n_pairs=n_pairs)

    def body(cur, prv):
        up_b, up_c, up_hid, act_down = cur
        ew_b, ew_c, ew_hid, act_ew = prv
        _sc_elementwise(ew_b, ew_c, ew_hid, act_ew, cw_ref, uhalo_ref, chunk, seq_start)
        h = h_ref[...]
        up_b[...] = jnp.dot(h, wb_ref[...], preferred_element_type=F32)
        up_c[...] = jnp.dot(h, wc_ref[...], preferred_element_type=F32)
        up_hid[...] = jnp.dot(h, wh_ref[...], preferred_element_type=F32)
        o_ref[...] += jnp.dot(act_down[...], w2_ref[...], preferred_element_type=F32)

    _by_parity(body, (b0_ref, c0_ref, hid0_ref, a0_ref), (b1_ref, c1_ref, hid1_ref, a1_ref))


def _short_conv(x, nw, w_in, conv_w, w_out, *, seq, tm, tc):
    t, d = x.shape
    sc_dim = w_out.shape[0]
    nch = sc_dim // tc
    n_tiles = t // tm
    assert nch >= 3 and tc % min(COL_BLOCK, tc) == 0 and tm % (2 * ROW_TILE) == 0
    up_tile, up_chunk, ew_chunk, down_chunk, down_tile = _mlp_index_maps(nch, n_tiles)
    kern = functools.partial(_sc_kernel, nch=nch, n_pairs=n_tiles * nch, tiles_per_seq=seq // tm)
    up_buf = pltpu.VMEM((tm, tc), F32)
    return pl.pallas_call(
        kern,
        grid=(n_tiles * nch + 2,),
        in_specs=[
            pl.BlockSpec((tm, d), lambda s: (up_tile(s), 0)),
            pl.BlockSpec((1, d), lambda s: (0, 0)),
            pl.BlockSpec((d, tc), lambda s: (0, up_chunk(s))),
            pl.BlockSpec((d, tc), lambda s: (0, nch + up_chunk(s))),
            pl.BlockSpec((d, tc), lambda s: (0, 2 * nch + up_chunk(s))),
            pl.BlockSpec((nch, SC_WIDTH * ROW_TILE, tc), lambda s: (0, 0, 0)),
            pl.BlockSpec((tc, d), lambda s: (down_chunk(s), 0)),
        ],
        out_specs=pl.BlockSpec((tm, d), lambda s: (down_tile(s), 0)),
        out_shape=jax.ShapeDtypeStruct((t, d), F32),
        scratch_shapes=[
            pltpu.VMEM((tm, d), BF16),
            up_buf, up_buf, up_buf, pltpu.VMEM((tm, tc), BF16),
            up_buf, up_buf, up_buf, pltpu.VMEM((tm, tc), BF16),
            pltpu.VMEM((nch, ROW_TILE, tc), F32),
        ],
        compiler_params=pltpu.CompilerParams(
            dimension_semantics=("arbitrary",),
            vmem_limit_bytes=VMEM_LIMIT_BYTES),
        name="short_conv_mixer",
    )(x, nw, w_in, w_in, w_in, _tap_rows(conv_w).reshape(SC_WIDTH * ROW_TILE, nch, tc).transpose(1, 0, 2), w_out)


def _tile(n, pref):
    c = min(pref, n)
    while n % c:
        c -= LANES
    return c


def _trunk(x, mix_norm_w, ffn_norm_w, final_norm_w,
           ssd_w_in, ssd_conv_w, ssd_conv_b, ssd_dt_bias, ssd_a_log, ssd_d, ssd_norm_w, ssd_w_out,
           sc_w_in, sc_conv_w, sc_w_out,
           ffn_w_up, ffn_conv_w, ffn_conv_b, ffn_w_down,
           *, head_dim=SSM_HEAD_DIM, n_groups=SSM_GROUPS, n_state=SSM_STATE,
           tm_proj=1024, tn_proj=1024, tm_mlp=512, tc_mlp=512):
    batch, seq, d_model = x.shape
    depth = mix_norm_w.shape[0]
    t = batch * seq
    d_inner = ssd_w_out.shape[1]
    n_heads = ssd_dt_bias.shape[1]
    main_cols = ssd_w_in.shape[2] - n_heads
    assert d_inner == n_heads * head_dim

    tm_proj = _tile(seq, tm_proj)
    tm_mlp = _tile(seq, tm_mlp)
    xs = x.reshape(t, d_model)
    for i in range(depth):
        j = i // 2
        nw = mix_norm_w[i][None, :]
        if i % 2 == 0:
            w_main = ssd_w_in[j, :, :main_cols].astype(BF16)
            w_dt = jnp.pad(ssd_w_in[j, :, main_cols:], ((0, 0), (0, LANES - n_heads))).astype(BF16)
            zxbc, dt_raw = _norm_inproj(xs, nw, w_main, w_dt, tm=tm_proj, tn=_tile(main_cols, tn_proj))
            y = _ssd_core(zxbc, dt_raw, ssd_conv_w[j], ssd_conv_b[j], ssd_dt_bias[j], ssd_a_log[j],
                          ssd_d[j], ssd_norm_w[j], batch=batch, seq=seq, d_inner=d_inner,
                          n_groups=n_groups, n_state=n_state, head_dim=head_dim)
            xs = _proj_residual(xs, y, ssd_w_out[j].astype(BF16), tm=tm_mlp, tn=_tile(d_model, tn_proj))
        else:
            xs = _short_conv(xs, nw, sc_w_in[j].astype(BF16), sc_conv_w[j], sc_w_out[j].astype(BF16),
                             seq=seq, tm=tm_mlp, tc=_tile(sc_w_out.shape[1], tc_mlp))
        xs = _conv_ffn(xs, ffn_norm_w[i][None, :], ffn_w_up[i].astype(BF16), ffn_conv_w[i],
                       ffn_conv_b[i][None, :], ffn_w_down[i].astype(BF16), final_norm_w[None, :],
                       seq=seq, tm=tm_mlp, tc=_tile(ffn_w_down.shape[1], tc_mlp),
                       final_norm=(i == depth - 1))
    return xs.reshape(batch, seq, d_model)


def kernel(x, mix_norm_w, ffn_norm_w, final_norm_w, ssd_w_in, ssd_conv_w, ssd_conv_b, ssd_dt_bias,
           ssd_a_log, ssd_d, ssd_norm_w, ssd_w_out, sc_w_in, sc_conv_w, sc_w_out,
           ffn_w_up, ffn_conv_w, ffn_conv_b, ffn_w_down):
    return _trunk(x, mix_norm_w, ffn_norm_w, final_norm_w,
                  ssd_w_in, ssd_conv_w, ssd_conv_b, ssd_dt_bias, ssd_a_log, ssd_d, ssd_norm_w, ssd_w_out,
                  sc_w_in, sc_conv_w, sc_w_out,
                  ffn_w_up, ffn_conv_w, ffn_conv_b, ffn_w_down)
```

```python
import functools

import jax
import jax.numpy as jnp
from jax import lax
from jax.experimental import pallas as pl
from jax.experimental.pallas import tpu as pltpu

F32 = jnp.float32
BF16 = jnp.bfloat16

EPS = 1e-5
SSM_HEAD_DIM = 64
SSM_GROUPS = 8
SSM_STATE = 128
SSM_CHUNK = 128
SSM_CONV = 4
SC_WIDTH = 3
FFN_CONV = 3

LANES = 128
SUBLANES = 8
VMEM_LIMIT_BYTES = 56 * 1024 * 1024

NEG_LOG2_E = -1.4426950408889634


def _sigmoid(v):
    return 1.0 / (1.0 + jnp.exp2(v * NEG_LOG2_E))


def _rmsnorm_rows(x_ref, w_ref, out_ref, rows_per_step=128):
    n_rows = x_ref.shape[0]
    w = w_ref[...]

    def body(r, carry):
        sl = pl.ds(pl.multiple_of(r * rows_per_step, rows_per_step), rows_per_step)
        x = x_ref[sl, :]
        ms = jnp.mean(x * x, axis=-1, keepdims=True)
        out_ref[sl, :] = (x * lax.rsqrt(ms + EPS) * w).astype(out_ref.dtype)
        return carry

    lax.fori_loop(0, n_rows // rows_per_step, body, 0)


ROW_TILE = SUBLANES
COL_BLOCK = 2 * LANES


class _ConvTaps:
    def __init__(self, taps, history):
        self.taps = taps
        shifts = range(1, len(taps))
        row = lax.broadcasted_iota(jnp.int32, history.shape, 0)
        self.keep = [row >= j for j in shifts]
        self.rolled = [pltpu.roll(history, j, 0) for j in shifts]

    def __call__(self, u):
        n = len(self.taps)
        rolled = [pltpu.roll(u, j, 0) for j in range(1, n)]
        acc = self.taps[n - 1] * u
        for j in range(1, n):
            acc = acc + self.taps[n - 1 - j] * jnp.where(self.keep[j - 1], rolled[j - 1], self.rolled[j - 1])
        self.rolled = rolled
        return acc


def _tap_tiles(par_ref, row0, n_taps, cols, chunk=None):
    lead = () if chunk is None else (chunk,)
    return [par_ref[lead + (slice(row0 + ROW_TILE * k, row0 + ROW_TILE * (k + 1)), cols)] for k in range(n_taps)]


def _tap_rows(conv_w):
    return jnp.repeat(conv_w, ROW_TILE, axis=0)


def _norm_inproj_kernel(x_ref, nw_ref, w_ref, wdt_ref, o_ref, dt_ref, h_ref):
    @pl.when(pl.program_id(1) == 0)
    def _():
        _rmsnorm_rows(x_ref, nw_ref, h_ref)
        dt_ref[...] = jnp.dot(h_ref[...], wdt_ref[...], preferred_element_type=F32)

    o_ref[...] = jnp.dot(h_ref[...], w_ref[...], preferred_element_type=F32).astype(o_ref.dtype)


def _norm_inproj(x, nw, w, wdt, *, tm, tn):
    t, d = x.shape
    n = w.shape[1]
    return pl.pallas_call(
        _norm_inproj_kernel,
        grid=(t // tm, n // tn),
        in_specs=[
            pl.BlockSpec((tm, d), lambda i, j: (i, 0)),
            pl.BlockSpec((1, d), lambda i, j: (0, 0)),
            pl.BlockSpec((d, tn), lambda i, j: (0, j)),
            pl.BlockSpec((d, LANES), lambda i, j: (0, 0)),
        ],
        out_specs=[
            pl.BlockSpec((tm, tn), lambda i, j: (i, j)),
            pl.BlockSpec((tm, LANES), lambda i, j: (i, 0)),
        ],
        out_shape=[
            jax.ShapeDtypeStruct((t, n), BF16),
            jax.ShapeDtypeStruct((t, LANES), F32),
        ],
        scratch_shapes=[pltpu.VMEM((tm, d), BF16)],
        compiler_params=pltpu.CompilerParams(
            dimension_semantics=("arbitrary", "arbitrary"),
            vmem_limit_bytes=VMEM_LIMIT_BYTES),
        name="ssd_norm_inproj",
    )(x, nw, w, wdt)


def _split_bf16(v, parts):
    out = []
    rem = v
    for _ in range(parts):
        p = rem.astype(BF16)
        out.append(p)
        rem = rem - p.astype(F32)
    return out


def _ssd_core_kernel(z_ref, x_ref, bc_ref, dt_ref, cpar_ref,
                     dtb_ref, alog_ref, dexp_ref, nw_ref, expand_ref, o_ref,
                     state_ref, hist_ref, xc_ref, bcc_ref, xdt_ref, xw_ref, ex_ref,
                     *, n_groups, heads_per_group, head_dim, n_state):
    L = SSM_CHUNK
    gw = heads_per_group * head_dim
    bdim = n_groups * n_state

    @pl.when(pl.program_id(1) == 0)
    def _():
        state_ref[...] = jnp.zeros_like(state_ref)
        hist_ref[...] = jnp.zeros_like(hist_ref)

    d_in = x_ref.shape[1]
    bias_row0 = SSM_CONV * ROW_TILE
    for src_ref, col0 in ((x_ref, 0), (bc_ref, d_in)):
        for c0 in range(0, src_ref.shape[1], COL_BLOCK):
            cols = slice(c0, c0 + COL_BLOCK)
            pcols = slice(col0 + c0, col0 + c0 + COL_BLOCK)
            conv = _ConvTaps(_tap_tiles(cpar_ref, 0, SSM_CONV, pcols), hist_ref[:, pcols])
            bias = cpar_ref[bias_row0:bias_row0 + ROW_TILE, pcols]
            tile = None
            for r0 in range(0, L, 2 * ROW_TILE):
                blk = src_ref[r0:r0 + 2 * ROW_TILE, cols].astype(F32)
                halves = []
                for tile in (blk[0:ROW_TILE], blk[ROW_TILE:2 * ROW_TILE]):
                    a = conv(tile) + bias
                    halves.append(a * _sigmoid(a))
                if src_ref is x_ref:
                    xc_ref[r0:r0 + 2 * ROW_TILE, cols] = jnp.concatenate(halves, axis=0)
                else:
                    bcc_ref[r0:r0 + 2 * ROW_TILE, cols] = jnp.concatenate(halves, axis=0).astype(BF16)
            hist_ref[:, pcols] = tile

    dtr = dt_ref[...] + dtb_ref[...]
    dtv = jnp.maximum(dtr, 0.0) + jnp.log(1.0 + jnp.exp(-jnp.abs(dtr)))
    a = -jnp.exp(alog_ref[...])
    da = dtv * a

    row = lax.broadcasted_iota(jnp.int32, (L, L), 0)
    col = lax.broadcasted_iota(jnp.int32, (L, L), 1)
    causal = row >= col
    tril = jnp.where(causal, 1.0, 0.0).astype(BF16)
    acs = jnp.dot(jnp.concatenate([tril, tril, tril], axis=1),
                  jnp.concatenate(_split_bf16(da, 3), axis=0),
                  preferred_element_type=F32)
    acs_t = acs.T
    last = acs[L - 1:L, :]
    dte = jnp.exp(last - acs)
    eacs = jnp.exp(acs)
    cdec = jnp.broadcast_to(jnp.exp(last), (2 * SUBLANES, LANES))

    q = jnp.concatenate([dtv, dtv * dte, cdec], axis=0)
    ex_ref[...] = jnp.dot(jnp.concatenate(_split_bf16(q, 2), axis=1), expand_ref[...],
                          preferred_element_type=F32)
    xc = xc_ref[...]
    xdt_ref[...] = (xc * ex_ref[0:L, :]).astype(BF16)
    xw_ref[...] = (xc * ex_ref[L:2 * L, :]).astype(BF16)

    lane = lax.broadcasted_iota(jnp.int32, (L, LANES), 1)
    lo_half = lane < head_dim
    heads_per_tile = LANES // head_dim
    tiles_per_group = gw // LANES
    neg_inf = jnp.float32(-jnp.inf)

    for g in range(n_groups):
        gs = slice(g * gw, (g + 1) * gw)
        bg = bcc_ref[:, g * n_state:(g + 1) * n_state]
        cg = bcc_ref[:, bdim + g * n_state:bdim + (g + 1) * n_state]
        cb = lax.dot_general(cg, bg, (((1,), (1,)), ((), ())), preferred_element_type=F32)
        cg32 = cg.astype(F32)
        prev = state_ref[:, gs]
        prev_bf = prev.astype(BF16)
        y_tiles = []
        for tl in range(tiles_per_group):
            lhs_parts = []
            rhs_parts = []
            xp = xdt_ref[:, g * gw + tl * LANES:g * gw + (tl + 1) * LANES]
            pp = prev_bf[:, tl * LANES:(tl + 1) * LANES]
            for hh in range(heads_per_tile):
                h = g * heads_per_group + tl * heads_per_tile + hh
                acol = acs[:, h:h + 1]
                arow = acs_t[h:h + 1, :]
                decay = jnp.exp(jnp.where(causal, acol - arow, neg_inf))
                lhs_parts.append((cb * decay).astype(BF16))
                lhs_parts.append((cg32 * eacs[:, h:h + 1]).astype(BF16))
                keep = lo_half if hh == 0 else jnp.logical_not(lo_half)
                rhs_parts.append(jnp.where(keep, xp, jnp.zeros_like(xp)))
                rhs_parts.append(jnp.where(keep, pp, jnp.zeros_like(pp)))
            y_tiles.append(jnp.dot(jnp.concatenate(lhs_parts, axis=1),
                                   jnp.concatenate(rhs_parts, axis=0),
                                   preferred_element_type=F32))
        y = jnp.concatenate(y_tiles, axis=1)

        s_new = lax.dot_general(bg, xw_ref[:, gs], (((0,), (0,)), ((), ())),
                                preferred_element_type=F32)
        state_ref[:, gs] = prev * ex_ref[2 * L:2 * L + 1, gs] + s_new

        yv = y + xc_ref[:, gs] * dexp_ref[:, gs]
        zg = z_ref[:, gs].astype(F32)
        yv = yv * (zg * _sigmoid(zg))
        ms = jnp.mean(yv * yv, axis=-1, keepdims=True)
        o_ref[:, gs] = (yv * lax.rsqrt(ms + EPS) * nw_ref[:, gs]).astype(o_ref.dtype)


def _ssd_core(zxbc, dt_raw, conv_w, conv_b, dt_bias, a_log, d_skip, norm_w, *,
              batch, seq, d_inner, n_groups, n_state, head_dim):
    t = zxbc.shape[0]
    n_heads = d_inner // head_dim
    heads_per_group = n_heads // n_groups
    bdim = n_groups * n_state
    L = SSM_CHUNK
    chunks = seq // L
    assert d_inner % (2 * bdim) == 0 and LANES % head_dim == 0 and n_heads <= LANES
    assert d_inner % COL_BLOCK == 0 and (2 * bdim) % COL_BLOCK == 0

    conv_dim = conv_w.shape[1]
    cpar = jnp.concatenate([_tap_rows(conv_w), jnp.broadcast_to(conv_b[None, :], (ROW_TILE, conv_dim))], axis=0)
    pad = LANES - n_heads
    dtb = jnp.pad(dt_bias, (0, pad))[None, :]
    alog = jnp.pad(a_log, (0, pad))[None, :]
    dexp = jnp.repeat(d_skip, head_dim)[None, :]
    nw = norm_w[None, :]
    e = (jnp.arange(LANES)[:, None] == (jnp.arange(d_inner) // head_dim)[None, :]).astype(BF16)
    expand = jnp.concatenate([e, e], axis=0)

    row_map = lambda b, c: (b * chunks + c, 0)
    const = lambda b, c: (0, 0)
    kern = functools.partial(_ssd_core_kernel, n_groups=n_groups, heads_per_group=heads_per_group,
                             head_dim=head_dim, n_state=n_state)
    return pl.pallas_call(
        kern,
        grid=(batch, chunks),
        in_specs=[
            pl.BlockSpec((L, d_inner), row_map),
            pl.BlockSpec((L, d_inner), lambda b, c: (b * chunks + c, 1)),
            pl.BlockSpec((L, 2 * bdim), lambda b, c: (b * chunks + c, d_inner // bdim)),
            pl.BlockSpec((L, LANES), row_map),
            pl.BlockSpec(((SSM_CONV + 1) * ROW_TILE, conv_dim), const),
            pl.BlockSpec((1, LANES), const),
            pl.BlockSpec((1, LANES), const),
            pl.BlockSpec((1, d_inner), const),
            pl.BlockSpec((1, d_inner), const),
            pl.BlockSpec((2 * LANES, d_inner), const),
        ],
        out_specs=pl.BlockSpec((L, d_inner), row_map),
        out_shape=jax.ShapeDtypeStruct((t, d_inner), BF16),
        scratch_shapes=[
            pltpu.VMEM((n_state, d_inner), F32),
            pltpu.VMEM((ROW_TILE, conv_dim), F32),
            pltpu.VMEM((L, d_inner), F32),
            pltpu.VMEM((L, 2 * bdim), BF16),
            pltpu.VMEM((L, d_inner), BF16),
            pltpu.VMEM((L, d_inner), BF16),
            pltpu.VMEM((2 * L + 2 * SUBLANES, d_inner), F32),
        ],
        compiler_params=pltpu.CompilerParams(
            dimension_semantics=("arbitrary", "arbitrary"),
            vmem_limit_bytes=VMEM_LIMIT_BYTES),
        name="ssd_core",
    )(zxbc, zxbc, zxbc, dt_raw, cpar, dtb, alog, dexp, nw, expand)


def _proj_residual_kernel(x_ref, y_ref, w_ref, o_ref):
    o_ref[...] = x_ref[...] + jnp.dot(y_ref[...], w_ref[...], preferred_element_type=F32)


def _proj_residual(x, y, w, *, tm, tn):
    t, d = x.shape
    k = y.shape[1]
    return pl.pallas_call(
        _proj_residual_kernel,
        grid=(d // tn, t // tm),
        in_specs=[
            pl.BlockSpec((tm, tn), lambda j, i: (i, j)),
            pl.BlockSpec((tm, k), lambda j, i: (i, 0)),
            pl.BlockSpec((k, tn), lambda j, i: (0, j)),
        ],
        out_specs=pl.BlockSpec((tm, tn), lambda j, i: (i, j)),
        out_shape=jax.ShapeDtypeStruct((t, d), F32),
        compiler_params=pltpu.CompilerParams(
            dimension_semantics=("arbitrary", "arbitrary"),
            vmem_limit_bytes=VMEM_LIMIT_BYTES),
        name="ssd_outproj_residual",
    )(x, y, w)


def _mlp_prologue(x_ref, nw_ref, o_ref, h_ref, zero_refs, *, nch, n_pairs):
    s = pl.program_id(0)

    @pl.when(s == 0)
    def _():
        for r in zero_refs:
            r[...] = jnp.zeros(r.shape, r.dtype)
        o_ref[...] = jnp.zeros(o_ref.shape, o_ref.dtype)

    @pl.when(jnp.logical_and(s % nch == 0, s < n_pairs))
    def _():
        _rmsnorm_rows(x_ref, nw_ref, h_ref)

    @pl.when(jnp.logical_and((s - 2) % nch == 0, s >= 2))
    def _():
        o_ref[...] = x_ref[...]


def _mlp_epilogue(o_ref, fnw_ref, *, nch, final_norm):
    if final_norm:
        s = pl.program_id(0)

        @pl.when(jnp.logical_and((s - 2) % nch == nch - 1, s >= 2))
        def _():
            _rmsnorm_rows(o_ref, fnw_ref, o_ref)


def _by_parity(body, set_a, set_b):
    s = pl.program_id(0)

    @pl.when(s % 2 == 0)
    def _():
        body(set_a, set_b)

    @pl.when(s % 2 == 1)
    def _():
        body(set_b, set_a)


def _elementwise_stage_ids(nch, tiles_per_seq):
    pair = jnp.maximum(pl.program_id(0) - 1, 0)
    return pair % nch, (pair // nch) % tiles_per_seq == 0


def _history(halo_ref, chunk, cols, seq_start):
    hist = halo_ref[chunk, :, cols]
    return jnp.where(seq_start, jnp.zeros_like(hist), hist)


def _zero_after(v):
    return jnp.minimum(jnp.abs(v), 0.0)


FFN_TAP_ROWS = FFN_CONV * ROW_TILE
FFN_PAR_ROWS = 2 * FFN_TAP_ROWS + 2 * ROW_TILE


def _ffn_elementwise(g_ref, v_ref, act_ref, par_ref, ghalo_ref, vhalo_ref, chunk, seq_start):
    tm, tc = g_ref.shape
    wait = None
    for c0 in range(0, tc, min(COL_BLOCK, tc)):
        cols = slice(c0, c0 + min(COL_BLOCK, tc))
        conv_g = _ConvTaps(_tap_tiles(par_ref, 0, FFN_CONV, cols, chunk),
                           _history(ghalo_ref, chunk, cols, seq_start))
        conv_v = _ConvTaps(_tap_tiles(par_ref, FFN_TAP_ROWS, FFN_CONV, cols, chunk),
                           _history(vhalo_ref, chunk, cols, seq_start))
        bias_g = par_ref[chunk, 2 * FFN_TAP_ROWS:2 * FFN_TAP_ROWS + ROW_TILE, cols]
        bias_v = par_ref[chunk, 2 * FFN_TAP_ROWS + ROW_TILE:FFN_PAR_ROWS, cols]
        for r0 in range(0, tm, 2 * ROW_TILE):
            halves = []
            for r in (r0, r0 + ROW_TILE):
                gt = g_ref[r:r + ROW_TILE, cols]
                vt = v_ref[r:r + ROW_TILE, cols]
                if wait is not None:
                    gt = gt + wait
                    vt = vt + wait
                g = conv_g(gt) + bias_g
                v = conv_v(vt) + bias_v
                den = 1.0 + jnp.exp2(g * NEG_LOG2_E)
                wait = _zero_after(den)
                halves.append(g * (1.0 / den) * v)
            act_ref[r0:r0 + 2 * ROW_TILE, cols] = jnp.concatenate(halves, axis=0).astype(BF16)
        ghalo_ref[chunk, :, cols] = g_ref[tm - ROW_TILE:tm, cols]
        vhalo_ref[chunk, :, cols] = v_ref[tm - ROW_TILE:tm, cols]


def _ffn_kernel(x_ref, nw_ref, wg_ref, wv_ref, par_ref, w2_ref, fnw_ref,
                o_ref, h_ref, g0_ref, v0_ref, a0_ref, g1_ref, v1_ref, a1_ref, ghalo_ref, vhalo_ref,
                *, nch, n_pairs, tiles_per_seq, final_norm):
    chunk, seq_start = _elementwise_stage_ids(nch, tiles_per_seq)
    _mlp_prologue(x_ref, nw_ref, o_ref, h_ref,
                  (g0_ref, v0_ref, a0_ref, g1_ref, v1_ref, a1_ref, ghalo_ref, vhalo_ref),
                  nch=nch, n_pairs=n_pairs)

    def body(cur, prv):
        up_g, up_v, act_down = cur
        ew_g, ew_v, act_ew = prv
        _ffn_elementwise(ew_g, ew_v, act_ew, par_ref, ghalo_ref, vhalo_ref, chunk, seq_start)
        h = h_ref[...]
        up_g[...] = jnp.dot(h, wg_ref[...], preferred_element_type=F32)
        up_v[...] = jnp.dot(h, wv_ref[...], preferred_element_type=F32)
        o_ref[...] += jnp.dot(act_down[...], w2_ref[...], preferred_element_type=F32)

    _by_parity(body, (g0_ref, v0_ref, a0_ref), (g1_ref, v1_ref, a1_ref))
    _mlp_epilogue(o_ref, fnw_ref, nch=nch, final_norm=final_norm)


def _mlp_index_maps(nch, n_tiles):
    up_tile = lambda s: jnp.minimum(s // nch, n_tiles - 1)
    up_chunk = lambda s: s % nch
    ew_chunk = lambda s: jnp.maximum(s - 1, 0) % nch
    down_chunk = lambda s: jnp.maximum(s - 2, 0) % nch
    down_tile = lambda s: jnp.maximum(s - 2, 0) // nch
    return up_tile, up_chunk, ew_chunk, down_chunk, down_tile


def _conv_ffn(x, nw, w_up, conv_w, conv_b, w_down, fnw, *, seq, tm, tc, final_norm):
    t, d = x.shape
    d_ff = w_down.shape[0]
    nch = d_ff // tc
    n_tiles = t // tm
    assert nch >= 3 and tc % min(COL_BLOCK, tc) == 0 and tm % (2 * ROW_TILE) == 0
    up_tile, up_chunk, ew_chunk, down_chunk, down_tile = _mlp_index_maps(nch, n_tiles)
    kern = functools.partial(_ffn_kernel, nch=nch, n_pairs=n_tiles * nch, tiles_per_seq=seq // tm,
                             final_norm=final_norm)
    taps = _tap_rows(conv_w).reshape(FFN_TAP_ROWS, 2 * nch, tc)
    bias = jnp.broadcast_to(conv_b, (ROW_TILE, conv_b.shape[1])).reshape(ROW_TILE, 2 * nch, tc)
    params = jnp.concatenate([taps[:, :nch], taps[:, nch:], bias[:, :nch], bias[:, nch:]], axis=0)
    params = params.transpose(1, 0, 2)
    return pl.pallas_call(
        kern,
        grid=(n_tiles * nch + 2,),
        in_specs=[
            pl.BlockSpec((tm, d), lambda s: (up_tile(s), 0)),
            pl.BlockSpec((1, d), lambda s: (0, 0)),
            pl.BlockSpec((d, tc), lambda s: (0, up_chunk(s))),
            pl.BlockSpec((d, tc), lambda s: (0, nch + up_chunk(s))),
            pl.BlockSpec((nch, FFN_PAR_ROWS, tc), lambda s: (0, 0, 0)),
            pl.BlockSpec((tc, d), lambda s: (down_chunk(s), 0)),
            pl.BlockSpec((1, d), lambda s: (0, 0)),
        ],
        out_specs=pl.BlockSpec((tm, d), lambda s: (down_tile(s), 0)),
        out_shape=jax.ShapeDtypeStruct((t, d), F32),
        scratch_shapes=[
            pltpu.VMEM((tm, d), BF16),
            pltpu.VMEM((tm, tc), F32), pltpu.VMEM((tm, tc), F32), pltpu.VMEM((tm, tc), BF16),
            pltpu.VMEM((tm, tc), F32), pltpu.VMEM((tm, tc), F32), pltpu.VMEM((tm, tc), BF16),
            pltpu.VMEM((nch, ROW_TILE, tc), F32),
            pltpu.VMEM((nch, ROW_TILE, tc), F32),
        ],
        compiler_params=pltpu.CompilerParams(
            dimension_semantics=("arbitrary",),
            vmem_limit_bytes=VMEM_LIMIT_BYTES),
        name="conv_ffn",
    )(x, nw, w_up, w_up, params, w_down, fnw)


def _sc_elementwise(b_ref, c_ref, hid_ref, act_ref, cw_ref, uhalo_ref, chunk, seq_start):
    tm, tc = b_ref.shape
    wait = None
    for c0 in range(0, tc, min(COL_BLOCK, tc)):
        cols = slice(c0, c0 + min(COL_BLOCK, tc))
        conv = _ConvTaps(_tap_tiles(cw_ref, 0, SC_WIDTH, cols, chunk), _history(uhalo_ref, chunk, cols, seq_start))
        last = None
        for r0 in range(0, tm, 2 * ROW_TILE):
            halves = []
            for r in (r0, r0 + ROW_TILE):
                ct = c_ref[r:r + ROW_TILE, cols]
                if wait is not None:
                    ct = ct + wait
                last = c_ref[r:r + ROW_TILE, cols] * hid_ref[r:r + ROW_TILE, cols]
                cv = conv(ct * hid_ref[r:r + ROW_TILE, cols])
                wait = _zero_after(cv)
                halves.append(b_ref[r:r + ROW_TILE, cols] * cv)
            act_ref[r0:r0 + 2 * ROW_TILE, cols] = jnp.concatenate(halves, axis=0).astype(BF16)
        uhalo_ref[chunk, :, cols] = last


def _sc_kernel(x_ref, nw_ref, wb_ref, wc_ref, wh_ref, cw_ref, w2_ref,
               o_ref, h_ref, b0_ref, c0_ref, hid0_ref, a0_ref, b1_ref, c1_ref, hid1_ref, a1_ref, uhalo_ref,
               *, nch, n_pairs, tiles_per_seq):
    chunk, seq_start = _elementwise_stage_ids(nch, tiles_per_seq)
    _mlp_prologue(x_ref, nw_ref, o_ref, h_ref,
                  (b0_ref, c0_ref, hid0_ref, a0_ref, b1_ref, c1_ref, hid1_ref, a1_ref, uhalo_ref),
                  nch=nch, n_pairs=n_pairs)

    def body(cur, prv):
        up_b, up_c, up_hid, act_down = cur
        ew_b, ew_c, ew_hid, act_ew = prv
        _sc_elementwise(ew_b, ew_c, ew_hid, act_ew, cw_ref, uhalo_ref, chunk, seq_start)
        h = h_ref[...]
        up_b[...] = jnp.dot(h, wb_ref[...], preferred_element_type=F32)
        up_c[...] = jnp.dot(h, wc_ref[...], preferred_element_type=F32)
        up_hid[...] = jnp.dot(h, wh_ref[...], preferred_element_type=F32)
        o_ref[...] += jnp.dot(act_down[...], w2_ref[...], preferred_element_type=F32)

    _by_parity(body, (b0_ref, c0_ref, hid0_ref, a0_ref), (b1_ref, c1_ref, hid1_ref, a1_ref))


def _short_conv(x, nw, w_in, conv_w, w_out, *, seq, tm, tc):
    t, d = x.shape
    sc_dim = w_out.shape[0]
    nch = sc_dim // tc
    n_tiles = t // tm
    assert nch >= 3 and tc % min(COL_BLOCK, tc) == 0 and tm % (2 * ROW_TILE) == 0
    up_tile, up_chunk, ew_chunk, down_chunk, down_tile = _mlp_index_maps(nch, n_tiles)
    kern = functools.partial(_sc_kernel, nch=nch, n_pairs=n_tiles * nch, tiles_per_seq=seq // tm)
    up_buf = pltpu.VMEM((tm, tc), F32)
    return pl.pallas_call(
        kern,
        grid=(n_tiles * nch + 2,),
        in_specs=[
            pl.BlockSpec((tm, d), lambda s: (up_tile(s), 0)),
            pl.BlockSpec((1, d), lambda s: (0, 0)),
            pl.BlockSpec((d, tc), lambda s: (0, up_chunk(s))),
            pl.BlockSpec((d, tc), lambda s: (0, nch + up_chunk(s))),
            pl.BlockSpec((d, tc), lambda s: (0, 2 * nch + up_chunk(s))),
            pl.BlockSpec((nch, SC_WIDTH * ROW_TILE, tc), lambda s: (0, 0, 0)),
            pl.BlockSpec((tc, d), lambda s: (down_chunk(s), 0)),
        ],
        out_specs=pl.BlockSpec((tm, d), lambda s: (down_tile(s), 0)),
        out_shape=jax.ShapeDtypeStruct((t, d), F32),
        scratch_shapes=[
            pltpu.VMEM((tm, d), BF16),
            up_buf, up_buf, up_buf, pltpu.VMEM((tm, tc), BF16),
            up_buf, up_buf, up_buf, pltpu.VMEM((tm, tc), BF16),
            pltpu.VMEM((nch, ROW_TILE, tc), F32),
        ],
        compiler_params=pltpu.CompilerParams(
            dimension_semantics=("arbitrary",),
            vmem_limit_bytes=VMEM_LIMIT_BYTES),
        name="short_conv_mixer",
    )(x, nw, w_in, w_in, w_in, _tap_rows(conv_w).reshape(SC_WIDTH * ROW_TILE, nch, tc).transpose(1, 0, 2), w_out)


def _tile(n, pref):
    c = min(pref, n)
    while n % c:
        c -= LANES
    return c


def _trunk(x, mix_norm_w, ffn_norm_w, final_norm_w,
           ssd_w_in, ssd_conv_w, ssd_conv_b, ssd_dt_bias, ssd_a_log, ssd_d, ssd_norm_w, ssd_w_out,
           sc_w_in, sc_conv_w, sc_w_out,
           ffn_w_up, ffn_conv_w, ffn_conv_b, ffn_w_down,
           *, head_dim=SSM_HEAD_DIM, n_groups=SSM_GROUPS, n_state=SSM_STATE,
           tm_proj=1024, tn_proj=1024, tm_mlp=512, tc_mlp=512):
    batch, seq, d_model = x.shape
    depth = mix_norm_w.shape[0]
    t = batch * seq
    d_inner = ssd_w_out.shape[1]
    n_heads = ssd_dt_bias.shape[1]
    main_cols = ssd_w_in.shape[2] - n_heads
    assert d_inner == n_heads * head_dim

    tm_proj = _tile(seq, tm_proj)
    tm_mlp = _tile(seq, tm_mlp)
    xs = x.reshape(t, d_model)
    for i in range(depth):
        j = i // 2
        nw = mix_norm_w[i][None, :]
        if i % 2 == 0:
            w_main = ssd_w_in[j, :, :main_cols].astype(BF16)
            w_dt = jnp.pad(ssd_w_in[j, :, main_cols:], ((0, 0), (0, LANES - n_heads))).astype(BF16)
            zxbc, dt_raw = _norm_inproj(xs, nw, w_main, w_dt, tm=tm_proj, tn=_tile(main_cols, tn_proj))
            y = _ssd_core(zxbc, dt_raw, ssd_conv_w[j], ssd_conv_b[j], ssd_dt_bias[j], ssd_a_log[j],
                          ssd_d[j], ssd_norm_w[j], batch=batch, seq=seq, d_inner=d_inner,
                          n_groups=n_groups, n_state=n_state, head_dim=head_dim)
            xs = _proj_residual(xs, y, ssd_w_out[j].astype(BF16), tm=tm_mlp, tn=_tile(d_model, tn_proj))
        else:
            xs = _short_conv(xs, nw, sc_w_in[j].astype(BF16), sc_conv_w[j], sc_w_out[j].astype(BF16),
                             seq=seq, tm=tm_mlp, tc=_tile(sc_w_out.shape[1], tc_mlp))
        xs = _conv_ffn(xs, ffn_norm_w[i][None, :], ffn_w_up[i].astype(BF16), ffn_conv_w[i],
                       ffn_conv_b[i][None, :], ffn_w_down[i].astype(BF16), final_norm_w[None, :],
                       seq=seq, tm=tm_mlp, tc=_tile(ffn_w_down.shape[1], tc_mlp),
                       final_norm=(i == depth - 1))
    return xs.reshape(batch, seq, d_model)


def kernel(x, mix_norm_w, ffn_norm_w, final_norm_w, ssd_w_in, ssd_conv_w, ssd_conv_b, ssd_dt_bias,
           ssd_a_log, ssd_d, ssd_norm_w, ssd_w_out, sc_w_in, sc_conv_w, sc_w_out,
           ffn_w_up, ffn_conv_w, ffn_conv_b, ffn_w_down):
    return _trunk(x, mix_norm_w, ffn_norm_w, final_norm_w,
                  ssd_w_in, ssd_conv_w, ssd_conv_b, ssd_dt_bias, ssd_a_log, ssd_d, ssd_norm_w, ssd_w_out,
                  sc_w_in, sc_conv_w, sc_w_out,
                  ffn_w_up, ffn_conv_w, ffn_conv_b, ffn_w_down)
```
